```python
import jax, jax.numpy as jnp
from jax import lax
import numpy as np

D_MODEL = 1024
BATCH = 1
SEQ = 16384
DEPTH = 4

D_MIX = D_MODEL
A_WIDTH = D_MIX // 2
A_HEADS = 8
A_HEAD_DIM = A_WIDTH // A_HEADS
A_CHUNK = 128
B_WIDTH = D_MIX - A_WIDTH
CONV_WIDTH = 3
PROJ_AB = 2 * A_WIDTH + 3 * B_WIDTH
C_GROUPS = 4
C_WINDOWS = (2, 4, 8, 16)
C_GROUP_DIM = D_MIX // C_GROUPS
N_EXPERTS = 32
TOP_K = 4
D_FF = D_MODEL
SWIGLU_LIMIT = 7.0
SWIGLU_ALPHA = 1.702
MOE_BLOCK = 256
LN_EPS = 1e-5
DEEPNORM_ALPHA = float((2 * DEPTH) ** 0.25)
DEEPNORM_BETA = float((8 * DEPTH) ** -0.25)
N_EVEN = (DEPTH + 1) // 2
N_ODD = DEPTH // 2

kernel_name = "hybrid_gmlp_shortconv_pool_moe_deepnorm"


def layer_norm(x, g, b=None):
    xf = x.astype(jnp.float32)
    mu = jnp.mean(xf, axis=-1, keepdims=True)
    var = jnp.mean(jnp.square(xf - mu), axis=-1, keepdims=True)
    y = (xf - mu) * lax.rsqrt(var + LN_EPS) * g.astype(jnp.float32)
    if b is not None:
        y = y + b.astype(jnp.float32)
    return y.astype(x.dtype)


def spatial_gating(u, v, norm_g, w_s, b_s):
    bsz, s, _ = v.shape
    v = layer_norm(v, norm_g)
    mask = jnp.tril(jnp.ones((A_CHUNK, A_CHUNK), dtype=bool))
    w = jnp.where(mask, w_s, jnp.zeros((), w_s.dtype))
    vc = v.reshape(bsz, s // A_CHUNK, A_CHUNK, A_HEADS, A_HEAD_DIM)
    sv = jnp.einsum('hts,bnshd->bnthd', w, vc) + b_s.T[None, None, :, :, None]
    return u * sv.reshape(bsz, s, A_WIDTH)


def short_gated_conv(g_b, g_c, x_in, conv_w):
    z = g_c * x_in
    s = z.shape[1]
    zp = jnp.pad(z, ((0, 0), (CONV_WIDTH - 1, 0), (0, 0)))
    conv = conv_w[0] * zp[:, 0:s]
    for k in range(1, CONV_WIDTH):
        conv = conv + conv_w[k] * zp[:, k:k + s]
    return g_b * conv


def multiscale_pool(z, w_grp, scale):
    bsz, s, _ = z.shape
    zf = z.astype(jnp.float32).reshape(bsz, s, C_GROUPS, C_GROUP_DIM)
    cs = jnp.cumsum(zf, axis=1)
    pos = jnp.arange(s)
    outs = []
    for g, win in enumerate(C_WINDOWS):
        csg = cs[:, :, g]
        lag = jnp.pad(csg, ((0, 0), (win, 0), (0, 0)))[:, :s]
        cnt = jnp.minimum(pos + 1, win).astype(jnp.float32)[None, :, None]
        outs.append((csg - lag) / cnt - zf[:, :, g])
    pooled = jnp.stack(outs, axis=2).astype(z.dtype)
    mixed = jnp.einsum('bsgc,gcd->bsgd', pooled, w_grp)
    return mixed.reshape(bsz, s, D_MIX) * scale


def clamped_swiglu(h):
    gate, lin = h[..., :D_FF], h[..., D_FF:]
    gate = jnp.minimum(gate, SWIGLU_LIMIT)
    lin = jnp.clip(lin, -SWIGLU_LIMIT, SWIGLU_LIMIT)
    return (lin + 1.0) * gate * jax.nn.sigmoid(SWIGLU_ALPHA * gate)


def moe_ffn(x, router_w, router_b, w_up, b_up, w_down, b_down):
    bsz, s, d = x.shape
    xt = x.reshape(-1, d)
    n_tok = xt.shape[0]
    logits = (xt @ router_w + router_b).astype(jnp.float32)
    top_val, top_idx = lax.top_k(logits, TOP_K)
    gates = jax.nn.softmax(top_val, axis=-1).astype(x.dtype)

    n_pair = n_tok * TOP_K
    flat_e = top_idx.reshape(-1)
    order = jnp.argsort(flat_e)
    e_sorted = flat_e[order]
    tok_sorted = order // TOP_K
    counts = jnp.bincount(flat_e, length=N_EXPERTS)
    starts = jnp.cumsum(counts) - counts
    padded_counts = (counts + MOE_BLOCK - 1) // MOE_BLOCK * MOE_BLOCK
    padded_ends = jnp.cumsum(padded_counts)
    padded_starts = padded_ends - padded_counts
    dest = padded_starts[e_sorted] + (jnp.arange(n_pair) - starts[e_sorted])
    n_blocks = -(-n_pair // MOE_BLOCK) + N_EXPERTS
    x_pad = jnp.zeros((n_blocks * MOE_BLOCK, d), x.dtype).at[dest].set(xt[tok_sorted])
    block_start = jnp.arange(n_blocks) * MOE_BLOCK
    block_expert = jnp.minimum(
        jnp.searchsorted(padded_ends, block_start, side='right'), N_EXPERTS - 1)

    def expert_block(args):
        xb, e = args
        h = xb @ w_up[e] + b_up[e]
        return clamped_swiglu(h) @ w_down[e] + b_down[e]

    y_pad = lax.map(expert_block, (x_pad.reshape(n_blocks, MOE_BLOCK, d), block_expert))
    y_sorted = y_pad.reshape(-1, d)[dest]
    y_pairs = jnp.zeros_like(y_sorted).at[order].set(y_sorted).reshape(n_tok, TOP_K, d)
    return jnp.einsum('tkd,tk->td', y_pairs, gates).reshape(bsz, s, d)


def setup_inputs(seed: int = 0) -> dict:
    key = jax.random.key(seed)
    ks = jax.random.split(key, 24)
    f32 = jnp.float32
    nrm = lambda k, shape, scale: jax.random.normal(k, shape, f32) * scale
    d_in = D_MODEL ** -0.5
    return {
        "x": nrm(ks[0], (BATCH, SEQ, D_MODEL), 1.0),
        "ab_w_in": nrm(ks[1], (N_EVEN, D_MODEL, PROJ_AB), d_in),
        "a_norm_g": 1.0 + nrm(ks[2], (N_EVEN, A_WIDTH), 0.05),
        "a_w_s": nrm(ks[3], (N_EVEN, A_HEADS, A_CHUNK, A_CHUNK), A_CHUNK ** -0.5),
        "a_b_s": 1.0 + nrm(ks[4], (N_EVEN, A_HEADS, A_CHUNK), 0.05),
        "b_conv_w": nrm(ks[5], (N_EVEN, CONV_WIDTH, B_WIDTH), CONV_WIDTH ** -0.5),
        "ab_w_out": nrm(ks[6], (N_EVEN, D_MIX, D_MODEL), D_MIX ** -0.5 * DEEPNORM_BETA),
        "c_w_in": nrm(ks[7], (N_ODD, D_MODEL, D_MIX), d_in),
        "c_w_grp": nrm(ks[8], (N_ODD, C_GROUPS, C_GROUP_DIM, C_GROUP_DIM), C_GROUP_DIM ** -0.5),
        "c_scale": 1.0 + nrm(ks[9], (N_ODD, D_MIX), 0.05),
        "c_w_out": nrm(ks[10], (N_ODD, D_MIX, D_MODEL), D_MIX ** -0.5 * DEEPNORM_BETA),
        "ln_mix_g": 1.0 + nrm(ks[11], (DEPTH, D_MODEL), 0.05),
        "ln_mix_b": nrm(ks[12], (DEPTH, D_MODEL), 0.02),
        "router_w": nrm(ks[13], (DEPTH, D_MODEL, N_EXPERTS), d_in),
        "router_b": nrm(ks[14], (DEPTH, N_EXPERTS), 0.01),
        "moe_w_up": nrm(ks[15], (DEPTH, N_EXPERTS, D_MODEL, 2 * D_FF), d_in),
        "moe_b_up": nrm(ks[16], (DEPTH, N_EXPERTS, 2 * D_FF), 0.02),
        "moe_w_down": nrm(ks[17], (DEPTH, N_EXPERTS, D_FF, D_MODEL), D_FF ** -0.5 * DEEPNORM_BETA),
        "moe_b_down": nrm(ks[18], (DEPTH, N_EXPERTS, D_MODEL), 0.02),
        "ln_ffn_g": 1.0 + nrm(ks[19], (DEPTH, D_MODEL), 0.05),
        "ln_ffn_b": nrm(ks[20], (DEPTH, D_MODEL), 0.02),
    }


def reference(x, ab_w_in, a_norm_g, a_w_s, a_b_s, b_conv_w, ab_w_out,
              c_w_in, c_w_grp, c_scale, c_w_out, ln_mix_g, ln_mix_b,
              router_w, router_b, moe_w_up, moe_b_up, moe_w_down, moe_b_down,
              ln_ffn_g, ln_ffn_b):
    splits = [A_WIDTH, 2 * A_WIDTH, 2 * A_WIDTH + B_WIDTH, 2 * A_WIDTH + 2 * B_WIDTH]
    for layer in range(DEPTH):
        i = layer // 2
        if layer % 2 == 0:
            z = x @ ab_w_in[i]
            u, v, g_b, g_c, x_in = jnp.split(z, splits, axis=-1)
            y_a = spatial_gating(jax.nn.gelu(u, approximate=False),
                                 jax.nn.gelu(v, approximate=False),
                                 a_norm_g[i], a_w_s[i], a_b_s[i])
            y_b = short_gated_conv(g_b, g_c, x_in, b_conv_w[i])
            mix = jnp.concatenate([y_a, y_b], axis=-1) @ ab_w_out[i]
        else:
            mix = multiscale_pool(x @ c_w_in[i], c_w_grp[i], c_scale[i]) @ c_w_out[i]
        x = layer_norm(DEEPNORM_ALPHA * x + mix, ln_mix_g[layer], ln_mix_b[layer])
        ffn = moe_ffn(x, router_w[layer], router_b[layer], moe_w_up[layer],
                      moe_b_up[layer], moe_w_down[layer], moe_b_down[layer])
        x = layer_norm(DEEPNORM_ALPHA * x + ffn, ln_ffn_g[layer], ln_ffn_b[layer])
    return x
```

```python
import functools
import math

import jax
import jax.numpy as jnp
from jax import lax
from jax.experimental import pallas as pl
from jax.experimental.pallas import tpu as pltpu

F32 = jnp.float32
BF16 = jnp.bfloat16
I32 = jnp.int32

D_MODEL = 1024
DEPTH = 4
A_WIDTH = 512
A_HEADS = 8
A_HEAD_DIM = 64
A_CHUNK = 128
B_WIDTH = 512
PROJ_AB = 2 * A_WIDTH + 3 * B_WIDTH
C_GROUPS = 4
C_WINDOWS = (2, 4, 8, 16)
C_GROUP_DIM = 256
N_EXPERTS = 32
TOP_K = 4
D_FF = 1024
SWIGLU_LIMIT = 7.0
SWIGLU_ALPHA = 1.702
LN_EPS = 1e-5
DEEPNORM_ALPHA = float((2 * DEPTH) ** 0.25)

SUBLANES = 8
LANES = 128
MIX_TILE = 512
CONV_HALO = 8
POOL_HALO = 16
ROUTE_TILE = 256
EXPERT_BLOCK = 256
LOCAL_ROWS = ROUTE_TILE * TOP_K + N_EXPERTS * SUBLANES
VMEM_LIMIT = 56 * 1024 * 1024


def _layer_norm(x, g, b=None):
    mu = jnp.mean(x, axis=-1, keepdims=True)
    xc = x - mu
    var = jnp.mean(xc * xc, axis=-1, keepdims=True)
    y = xc * lax.rsqrt(var + LN_EPS) * g
    if b is not None:
        y = y + b
    return y


def _gelu_exact(x):
    return 0.5 * x * (1.0 + lax.erf(x * (1.0 / math.sqrt(2.0))))


def _even_mixer_kernel(x_ref, win_ref, ng_ref, wcat_ref, bias_ref, cw_ref, wout_ref,
                       lg_ref, lb_ref, o_ref, zc_ref):
    i = pl.program_id(0)
    tm = x_ref.shape[0]
    x = x_ref[...]
    z = jnp.dot(x.astype(BF16), win_ref[...], preferred_element_type=F32)
    u = _gelu_exact(z[:, :A_WIDTH])
    v = _gelu_exact(z[:, A_WIDTH:2 * A_WIDTH])
    g_b = z[:, 2 * A_WIDTH:2 * A_WIDTH + B_WIDTH]
    g_c = z[:, 2 * A_WIDTH + B_WIDTH:2 * A_WIDTH + 2 * B_WIDTH]
    x_in = z[:, 2 * A_WIDTH + 2 * B_WIDTH:]

    vb = _layer_norm(v, ng_ref[...]).astype(BF16)
    first_head = lax.broadcasted_iota(I32, (A_CHUNK, LANES), 1) < A_HEAD_DIM
    zero = jnp.zeros((), BF16)
    chunks = []
    for c in range(tm // A_CHUNK):
        slabs = []
        for j in range(A_WIDTH // LANES):
            slab = vb[c * A_CHUNK:(c + 1) * A_CHUNK, j * LANES:(j + 1) * LANES]
            rhs = jnp.concatenate([jnp.where(first_head, slab, zero),
                                   jnp.where(first_head, zero, slab)], axis=0)
            slabs.append(jnp.dot(wcat_ref[j], rhs, preferred_element_type=F32))
        chunks.append(jnp.concatenate(slabs, axis=1) + bias_ref[...])
    y_a = u * jnp.concatenate(chunks, axis=0)

    zz = g_c * x_in

    @pl.when(i == 0)
    def _():
        zc_ref[0:CONV_HALO, :] = jnp.zeros((CONV_HALO, B_WIDTH), F32)

    zc_ref[CONV_HALO:CONV_HALO + tm, :] = zz
    z1 = zc_ref[CONV_HALO - 1:CONV_HALO - 1 + tm, :]
    z2 = zc_ref[CONV_HALO - 2:CONV_HALO - 2 + tm, :]
    cw = cw_ref[...]
    conv = cw[0:1, :] * z2 + cw[1:2, :] * z1 + cw[2:3, :] * zz
    y_b = g_b * conv
    zc_ref[0:CONV_HALO, :] = zc_ref[tm:tm + CONV_HALO, :]

    y = jnp.concatenate([y_a, y_b], axis=1).astype(BF16)
    mix = jnp.dot(y, wout_ref[...], preferred_element_type=F32)
    o_ref[...] = _layer_norm(DEEPNORM_ALPHA * x + mix, lg_ref[...], lb_ref[...])


def _odd_mixer_kernel(x_ref, win_ref, wgrp_ref, scale_ref, wout_ref, lg_ref, lb_ref,
                      o_ref, zh_ref):
    i = pl.program_id(0)
    tm = x_ref.shape[0]
    x = x_ref[...]
    z = jnp.dot(x.astype(BF16), win_ref[...], preferred_element_type=F32)

    @pl.when(i == 0)
    def _():
        zh_ref[0:POOL_HALO, :] = jnp.zeros((POOL_HALO, D_MODEL), F32)

    zh_ref[POOL_HALO:POOL_HALO + tm, :] = z
    pos = i * tm + lax.broadcasted_iota(I32, (tm, 1), 0)
    mixed = []
    for g, win in enumerate(C_WINDOWS):
        cols = slice(g * C_GROUP_DIM, (g + 1) * C_GROUP_DIM)
        acc = zh_ref[:, cols]
        shift = 1
        while shift < win:
            acc = acc + pltpu.roll(acc, shift, axis=0)
            shift *= 2
        cnt = jnp.minimum(pos + 1, win).astype(F32)
        pooled = acc[POOL_HALO:, :] / cnt - z[:, cols]
        mixed.append(jnp.dot(pooled.astype(BF16), wgrp_ref[g], preferred_element_type=F32))
    zh_ref[0:POOL_HALO, :] = zh_ref[tm:tm + POOL_HALO, :]
    y = (jnp.concatenate(mixed, axis=1) * scale_ref[...]).astype(BF16)
    mix = jnp.dot(y, wout_ref[...], preferred_element_type=F32)
    o_ref[...] = _layer_norm(DEEPNORM_ALPHA * x + mix, lg_ref[...], lb_ref[...])


def _full(shape):
    return pl.BlockSpec(shape, lambda i: (0,) * len(shape))


def _even_mixer(x, w_in, ng, wcat, bias, cw, w_out, lg, lb):
    t = x.shape[0]
    return pl.pallas_call(
        _even_mixer_kernel,
        grid=(t // MIX_TILE,),
        in_specs=[pl.BlockSpec((MIX_TILE, D_MODEL), lambda i: (i, 0)),
                  _full(w_in.shape), _full(ng.shape), _full(wcat.shape), _full(bias.shape),
                  _full(cw.shape), _full(w_out.shape), _full(lg.shape), _full(lb.shape)],
        out_specs=pl.BlockSpec((MIX_TILE, D_MODEL), lambda i: (i, 0)),
        out_shape=jax.ShapeDtypeStruct((t, D_MODEL), F32),
        scratch_shapes=[pltpu.VMEM((MIX_TILE + CONV_HALO, B_WIDTH), F32)],
        compiler_params=pltpu.CompilerParams(dimension_semantics=("arbitrary",),
                                             vmem_limit_bytes=VMEM_LIMIT),
        name="even_mixer",
    )(x, w_in, ng, wcat, bias, cw, w_out, lg, lb)


def _odd_mixer(x, w_in, w_grp, scale, w_out, lg, lb):
    t = x.shape[0]
    return pl.pallas_call(
        _odd_mixer_kernel,
        grid=(t // MIX_TILE,),
        in_specs=[pl.BlockSpec((MIX_TILE, D_MODEL), lambda i: (i, 0)),
                  _full(w_in.shape), _full(w_grp.shape), _full(scale.shape),
                  _full(w_out.shape), _full(lg.shape), _full(lb.shape)],
        out_specs=pl.BlockSpec((MIX_TILE, D_MODEL), lambda i: (i, 0)),
        out_shape=jax.ShapeDtypeStruct((t, D_MODEL), F32),
        scratch_shapes=[pltpu.VMEM((MIX_TILE + POOL_HALO, D_MODEL), F32)],
        compiler_params=pltpu.CompilerParams(dimension_semantics=("arbitrary",),
                                             vmem_limit_bytes=VMEM_LIMIT),
        name="odd_mixer",
    )(x, w_in, w_grp, scale, w_out, lg, lb)


def _route_kernel(x_ref, rwt_ref, rb_ref, upper_ref, lower_ref, ls_ref, g_ref, c8_ref):
    td = x_ref.shape[0]
    logits = lax.dot_general(rwt_ref[...], x_ref[...], (((1,), (1,)), ((), ())),
                             precision=lax.Precision.HIGHEST,
                             preferred_element_type=F32) + rb_ref[...]
    eio = lax.broadcasted_iota(I32, (N_EXPERTS, td), 0)
    work = logits
    picks, vals = [], []
    for _ in range(TOP_K):
        m = jnp.max(work, axis=0, keepdims=True)
        idx = jnp.min(jnp.where(work == m, eio, N_EXPERTS), axis=0, keepdims=True)
        pick = eio == idx
        picks.append(pick)
        vals.append(m)
        work = jnp.where(pick, -jnp.inf, work)
    ex = [jnp.exp(v - vals[0]) for v in vals]
    den = ex[0] + ex[1] + ex[2] + ex[3]
    gates = [e / den for e in ex]

    onehot = jnp.zeros((N_EXPERTS, td), F32)
    for pick in picks:
        onehot = onehot + pick.astype(F32)
    rank = jnp.dot(onehot.astype(BF16), upper_ref[...], preferred_element_type=F32)
    cnt = jnp.sum(onehot, axis=1, keepdims=True).astype(I32)
    c8 = ((cnt + (SUBLANES - 1)) // SUBLANES) * SUBLANES
    c8b = jnp.broadcast_to(c8.astype(F32), (N_EXPERTS, LANES))
    loff = jnp.dot(lower_ref[...], c8b.astype(BF16), preferred_element_type=F32)[:, 0:1]
    slot = rank + loff
    rows = [jnp.sum(jnp.where(pick, slot, 0.0), axis=0, keepdims=True) for pick in picks]
    pad = jnp.zeros((SUBLANES - TOP_K, td), F32)
    ls_ref[...] = jnp.concatenate(rows + [pad], axis=0).astype(I32)
    g_ref[...] = jnp.concatenate(gates + [pad], axis=0)
    c8_ref[...] = jnp.broadcast_to(c8, (N_EXPERTS, LANES))[None]


def _route(x1, rwt, rb, upper, lower):
    t = x1.shape[0]
    n_tiles = t // ROUTE_TILE
    return pl.pallas_call(
        _route_kernel,
        grid=(n_tiles,),
        in_specs=[pl.BlockSpec((ROUTE_TILE, D_MODEL), lambda i: (i, 0)),
                  _full(rwt.shape), _full(rb.shape), _full(upper.shape), _full(lower.shape)],
        out_specs=[pl.BlockSpec((SUBLANES, ROUTE_TILE), lambda i: (0, i)),
                   pl.BlockSpec((SUBLANES, ROUTE_TILE), lambda i: (0, i)),
                   pl.BlockSpec((1, N_EXPERTS, LANES), lambda i: (i, 0, 0))],
        out_shape=[jax.ShapeDtypeStruct((SUBLANES, t), I32),
                   jax.ShapeDtypeStruct((SUBLANES, t), F32),
                   jax.ShapeDtypeStruct((n_tiles, N_EXPERTS, LANES), I32)],
        compiler_params=pltpu.CompilerParams(dimension_semantics=("arbitrary",)),
        name="route",
    )(x1, rwt, rb, upper, lower)


def _slot_matches(ls_ref, k):
    s_iota = lax.broadcasted_iota(I32, (LOCAL_ROWS, ROUTE_TILE), 0)
    return s_iota == ls_ref[k:k + 1, :]


def _for_each_granule(c8_s, loff_s, gst_s, tile, fn):
    def per_expert(e, carry):
        idx = tile * N_EXPERTS + e
        lo = loff_s[idx]
        go = gst_s[idx]

        def per_granule(g, c):
            fn(pl.multiple_of(lo + g * SUBLANES, SUBLANES),
               pl.multiple_of(go + g * SUBLANES, SUBLANES))
            return c

        return lax.fori_loop(0, c8_s[idx] // SUBLANES, per_granule, carry)

    lax.fori_loop(0, N_EXPERTS, per_expert, 0)


def _dispatch_kernel(c8_s, loff_s, gst_s, ngr_s, tst_s, tn_s,
                     x_ref, ls_ref, xpad_ref, xs_ref, zero_ref, sem, zsem):
    i = pl.program_id(0)
    n = pl.num_programs(0)
    slot = i % 2

    def granule_copy(buf, src_row, dst_row):
        return pltpu.make_async_copy(xs_ref.at[buf, pl.ds(src_row, SUBLANES)],
                                     xpad_ref.at[pl.ds(dst_row, SUBLANES)], sem.at[buf])

    def wait_tile(tile, buf):
        def body(_, c):
            granule_copy(buf, 0, 0).wait()
            return c
        lax.fori_loop(0, ngr_s[tile], body, 0)

    def zero_copy(dst_row):
        return pltpu.make_async_copy(zero_ref, xpad_ref.at[pl.ds(dst_row, SUBLANES)], zsem)

    @pl.when(i == 0)
    def _():
        zero_ref[...] = jnp.zeros(zero_ref.shape, F32)

        def per_expert(e, total):
            def per_granule(g, c):
                zero_copy(pl.multiple_of(tst_s[e] + g * SUBLANES, SUBLANES)).start()
                return c
            lax.fori_loop(0, tn_s[e], per_granule, 0)
            return total + tn_s[e]

        total = lax.fori_loop(0, N_EXPERTS, per_expert, 0)

        def drain(_, c):
            zero_copy(0).wait()
            return c
        lax.fori_loop(0, total, drain, 0)

    match = _slot_matches(ls_ref, 0)
    for k in range(1, TOP_K):
        match = match | _slot_matches(ls_ref, k)
    perm = jnp.where(match, 1.0, 0.0).astype(BF16)
    xs_ref[slot] = jnp.dot(perm, x_ref[...].astype(BF16), preferred_element_type=F32)

    _for_each_granule(c8_s, loff_s, gst_s, i,
                      lambda lo, go: granule_copy(slot, lo, go).start())

    @pl.when(i > 0)
    def _():
        wait_tile(i - 1, 1 - slot)

    @pl.when(i == n - 1)
    def _():
        wait_tile(i, slot)


def _dispatch(tables, x1, lslot, n_rows):
    t = x1.shape[0]
    n_tiles = t // ROUTE_TILE
    grid_spec = pltpu.PrefetchScalarGridSpec(
        num_scalar_prefetch=len(tables),
        grid=(n_tiles,),
        in_specs=[pl.BlockSpec((ROUTE_TILE, D_MODEL), lambda i, *_: (i, 0)),
                  pl.BlockSpec((SUBLANES, ROUTE_TILE), lambda i, *_: (0, i))],
        out_specs=pl.BlockSpec(memory_space=pl.ANY),
        scratch_shapes=[pltpu.VMEM((2, LOCAL_ROWS, D_MODEL), F32),
                        pltpu.VMEM((SUBLANES, D_MODEL), F32),
                        pltpu.SemaphoreType.DMA((2,)),
                        pltpu.SemaphoreType.DMA],
    )
    return pl.pallas_call(
        _dispatch_kernel,
        grid_spec=grid_spec,
        out_shape=jax.ShapeDtypeStruct((n_rows, D_MODEL), F32),
        compiler_params=pltpu.CompilerParams(dimension_semantics=("arbitrary",),
                                             vmem_limit_bytes=VMEM_LIMIT),
        name="dispatch",
    )(*tables, x1, lslot)


def _expert_kernel(be_s, nu_s, x_ref, wup_ref, bup_ref, wdn_ref, bdn_ref, o_ref,
                   wup_bf, wdn_bf):
    b = pl.program_id(0)

    @pl.when(b < nu_s[0])
    def _():
        prev = be_s[jnp.maximum(b - 1, 0)]

        @pl.when((b == 0) | (be_s[b] != prev))
        def _():
            wup_bf[...] = wup_ref[...].astype(BF16)
            wdn_bf[...] = wdn_ref[...].astype(BF16)

        h = jnp.dot(x_ref[...].astype(BF16), wup_bf[...],
                    preferred_element_type=F32) + bup_ref[...]
        gate = jnp.minimum(h[:, :D_FF], SWIGLU_LIMIT)
        lin = jnp.clip(h[:, D_FF:], -SWIGLU_LIMIT, SWIGLU_LIMIT)
        act = (lin + 1.0) * gate * jax.nn.sigmoid(SWIGLU_ALPHA * gate)
        o_ref[...] = jnp.dot(act.astype(BF16), wdn_bf[...],
                             preferred_element_type=F32) + bdn_ref[...]


def _experts(layer, block_expert, n_used, x_pad, w_up, b_up, w_down, b_down):
    n_blocks = x_pad.shape[0] // EXPERT_BLOCK

    def row_map(b, be, nu):
        return (jnp.minimum(b, nu[0] - 1), 0)

    def w_map(b, be, nu):
        return (layer, be[b], 0, 0)

    grid_spec = pltpu.PrefetchScalarGridSpec(
        num_scalar_prefetch=2,
        grid=(n_blocks,),
        in_specs=[pl.BlockSpec((EXPERT_BLOCK, D_MODEL), row_map),
                  pl.BlockSpec((None, None, D_MODEL, 2 * D_FF), w_map),
                  pl.BlockSpec((None, None, 1, 2 * D_FF), w_map),
                  pl.BlockSpec((None, None, D_FF, D_MODEL), w_map),
                  pl.BlockSpec((None, None, 1, D_MODEL), w_map)],
        out_specs=pl.BlockSpec((EXPERT_BLOCK, D_MODEL), row_map),
        scratch_shapes=[pltpu.VMEM((D_MODEL, 2 * D_FF), BF16),
                        pltpu.VMEM((D_FF, D_MODEL), BF16)],
    )
    return pl.pallas_call(
        _expert_kernel,
        grid_spec=grid_spec,
        out_shape=jax.ShapeDtypeStruct(x_pad.shape, F32),
        compiler_params=pltpu.CompilerParams(dimension_semantics=("arbitrary",),
                                             vmem_limit_bytes=VMEM_LIMIT),
        name="experts",
    )(block_expert, n_used, x_pad, w_up, b_up, w_down, b_down)


def _combine_kernel(c8_s, loff_s, gst_s, ngr_s,
                    x_ref, ls_ref, g_ref, lg_ref, lb_ref, ypad_ref, o_ref, yl_ref, sem):
    i = pl.program_id(0)
    n = pl.num_programs(0)
    slot = i % 2

    def granule_copy(buf, src_row, dst_row):
        return pltpu.make_async_copy(ypad_ref.at[pl.ds(src_row, SUBLANES)],
                                     yl_ref.at[buf, pl.ds(dst_row, SUBLANES)], sem.at[buf])

    def fetch(tile, buf):
        _for_each_granule(c8_s, loff_s, gst_s, tile,
                          lambda lo, go: granule_copy(buf, go, lo).start())

    @pl.when(i == 0)
    def _():
        yl_ref[...] = jnp.zeros(yl_ref.shape, F32)
        fetch(0, 0)

    @pl.when(i + 1 < n)
    def _():
        fetch(i + 1, 1 - slot)

    def drain(_, c):
        granule_copy(slot, 0, 0).wait()
        return c
    lax.fori_loop(0, ngr_s[i], drain, 0)

    gated = jnp.zeros((LOCAL_ROWS, ROUTE_TILE), F32)
    for k in range(TOP_K):
        gated = gated + jnp.where(_slot_matches(ls_ref, k), g_ref[k:k + 1, :], 0.0)
    ffn = jnp.dot(gated.T.astype(BF16), yl_ref[slot].astype(BF16),
                  preferred_element_type=F32)
    o_ref[...] = _layer_norm(DEEPNORM_ALPHA * x_ref[...] + ffn, lg_ref[...], lb_ref[...])


def _combine(tables, x1, lslot, gates, lg, lb, y_pad):
    t = x1.shape[0]
    n_tiles = t // ROUTE_TILE
    grid_spec = pltpu.PrefetchScalarGridSpec(
        num_scalar_prefetch=len(tables),
        grid=(n_tiles,),
        in_specs=[pl.BlockSpec((ROUTE_TILE, D_MODEL), lambda i, *_: (i, 0)),
                  pl.BlockSpec((SUBLANES, ROUTE_TILE), lambda i, *_: (0, i)),
                  pl.BlockSpec((SUBLANES, ROUTE_TILE), lambda i, *_: (0, i)),
                  pl.BlockSpec(lg.shape, lambda i, *_: (0, 0)),
                  pl.BlockSpec(lb.shape, lambda i, *_: (0, 0)),
                  pl.BlockSpec(memory_space=pl.ANY)],
        out_specs=pl.BlockSpec((ROUTE_TILE, D_MODEL), lambda i, *_: (i, 0)),
        scratch_shapes=[pltpu.VMEM((2, LOCAL_ROWS, D_MODEL), F32),
                        pltpu.SemaphoreType.DMA((2,))],
    )
    return pl.pallas_call(
        _combine_kernel,
        grid_spec=grid_spec,
        out_shape=jax.ShapeDtypeStruct((t, D_MODEL), F32),
        compiler_params=pltpu.CompilerParams(dimension_semantics=("arbitrary",),
                                             vmem_limit_bytes=VMEM_LIMIT),
        name="combine",
    )(*tables, x1, lslot, gates, lg, lb, y_pad)


def _grouped_rows_bound(n_tokens):
    n_tiles = n_tokens // ROUTE_TILE
    worst = (n_tokens * TOP_K + n_tiles * N_EXPERTS * (SUBLANES - 1)
             + N_EXPERTS * (EXPERT_BLOCK - 1))
    return -(-worst // EXPERT_BLOCK) * EXPERT_BLOCK


def _routing_tables(c8, n_blocks):
    seg_total = jnp.sum(c8, axis=0)
    padded = (seg_total + EXPERT_BLOCK - 1) // EXPERT_BLOCK * EXPERT_BLOCK
    pend = jnp.cumsum(padded)
    pstart = pend - padded
    gstart = pstart[None, :] + jnp.cumsum(c8, axis=0) - c8
    loff = jnp.cumsum(c8, axis=1) - c8
    ngran = jnp.sum(c8, axis=1) // SUBLANES
    tail_start = pstart + seg_total
    tail_n = (padded - seg_total) // SUBLANES
    n_used = pend[-1] // EXPERT_BLOCK
    blk = jnp.minimum(jnp.arange(n_blocks, dtype=I32), n_used - 1) * EXPERT_BLOCK
    block_expert = jnp.minimum(jnp.searchsorted(pend, blk, side='right'), N_EXPERTS - 1)
    as_i32 = lambda a: a.astype(I32)
    tile_tables = (as_i32(c8.reshape(-1)), as_i32(loff.reshape(-1)),
                   as_i32(gstart.reshape(-1)), as_i32(ngran))
    return (tile_tables, as_i32(tail_start), as_i32(tail_n),
            as_i32(block_expert), as_i32(n_used.reshape(1)))


def _moe(layer, x1, rwt, rb, upper, lower, w_up, b_up, w_down, b_down, lg, lb):
    t = x1.shape[0]
    n_rows = _grouped_rows_bound(t)
    lslot, gates, c8 = _route(x1, rwt, rb, upper, lower)
    tile_tables, tail_start, tail_n, block_expert, n_used = _routing_tables(
        c8[:, :, 0], n_rows // EXPERT_BLOCK)
    x_pad = _dispatch(tile_tables + (tail_start, tail_n), x1, lslot, n_rows)
    y_pad = _experts(layer, block_expert, n_used, x_pad, w_up, b_up, w_down, b_down)
    return _combine(tile_tables, x1, lslot, gates, lg, lb, y_pad)


def kernel(x, ab_w_in, a_norm_g, a_w_s, a_b_s, b_conv_w, ab_w_out, c_w_in, c_w_grp, c_scale,
           c_w_out, ln_mix_g, ln_mix_b, router_w, router_b, moe_w_up, moe_b_up, moe_w_down,
           moe_b_down, ln_ffn_g, ln_ffn_b):
    bsz, seq, d = x.shape
    assert d == D_MODEL and (bsz * seq) % MIX_TILE == 0 and seq % A_CHUNK == 0
    assert bsz == 1, "token tiles carry conv / pooling state across the sequence"
    h = x.reshape(bsz * seq, d)

    upper = jnp.triu(jnp.ones((ROUTE_TILE, ROUTE_TILE), BF16), k=1)
    lower = jnp.tril(jnp.ones((N_EXPERTS, N_EXPERTS), BF16), k=-1)
    causal = jnp.tril(jnp.ones((A_CHUNK, A_CHUNK), dtype=bool))
    b_up = moe_b_up.reshape(DEPTH, N_EXPERTS, 1, 2 * D_FF)
    b_down = moe_b_down.reshape(DEPTH, N_EXPERTS, 1, D_MODEL)

    for layer in range(DEPTH):
        i = layer // 2
        lg = ln_mix_g[layer].reshape(1, d)
        lb = ln_mix_b[layer].reshape(1, d)
        if layer % 2 == 0:
            w_s = jnp.where(causal, a_w_s[i], 0.0).astype(BF16)
            wcat = w_s.reshape(A_HEADS // 2, 2, A_CHUNK, A_CHUNK).transpose(0, 2, 1, 3)
            wcat = wcat.reshape(A_HEADS // 2, A_CHUNK, 2 * A_CHUNK)
            bias = jnp.repeat(a_b_s[i].T, A_HEAD_DIM, axis=1)
            h = _even_mixer(h, ab_w_in[i].astype(BF16), a_norm_g[i].reshape(1, A_WIDTH),
                            wcat, bias, b_conv_w[i], ab_w_out[i].astype(BF16), lg, lb)
        else:
            h = _odd_mixer(h, c_w_in[i].astype(BF16), c_w_grp[i].astype(BF16),
                           c_scale[i].reshape(1, d), c_w_out[i].astype(BF16), lg, lb)
        h = _moe(layer, h, router_w[layer].T, router_b[layer].reshape(N_EXPERTS, 1),
                 upper, lower, moe_w_up, b_up, moe_w_down, b_down,
                 ln_ffn_g[layer].reshape(1, d), ln_ffn_b[layer].reshape(1, d))
    return h.reshape(bsz, seq, d)
```

```python
import functools
import math

import jax
import jax.numpy as jnp
from jax import lax
from jax.experimental import pallas as pl
from jax.experimental.pallas import tpu as pltpu

F32 = jnp.float32
BF16 = jnp.bfloat16
I32 = jnp.int32

D_MODEL = 1024
DEPTH = 4
A_WIDTH = 512
A_HEADS = 8
A_HEAD_DIM = 64
A_CHUNK = 128
B_WIDTH = 512
PROJ_AB = 2 * A_WIDTH + 3 * B_WIDTH
C_GROUPS = 4
C_WINDOWS = (2, 4, 8, 16)
C_GROUP_DIM = 256
N_EXPERTS = 32
TOP_K = 4
D_FF = 1024
SWIGLU_LIMIT = 7.0
SWIGLU_ALPHA = 1.702
LN_EPS = 1e-5
DEEPNORM_ALPHA = float((2 * DEPTH) ** 0.25)

SUBLANES = 8
LANES = 128
MIX_TILE = 512
CONV_HALO = 8
POOL_HALO = 16
ROUTE_TILE = 256
EXPERT_BLOCK = 256
LOCAL_ROWS = ROUTE_TILE * TOP_K + N_EXPERTS * SUBLANES
LOCAL_GRANULES = LOCAL_ROWS // SUBLANES
VMEM_LIMIT = 56 * 1024 * 1024


def _layer_norm(x, g, b=None):
    mu = jnp.mean(x, axis=-1, keepdims=True)
    xc = x - mu
    var = jnp.mean(xc * xc, axis=-1, keepdims=True)
    y = xc * lax.rsqrt(var + LN_EPS) * g
    if b is not None:
        y = y + b
    return y


def _gelu_exact(x):
    return 0.5 * x * (1.0 + lax.erf(x * (1.0 / math.sqrt(2.0))))


def _even_mixer_kernel(x_ref, win_ref, ng_ref, wcat_ref, bias_ref, cw_ref, wout_ref,
                       lg_ref, lb_ref, o_ref, zc_ref):
    i = pl.program_id(0)
    tm = x_ref.shape[0]
    x = x_ref[...]
    z = jnp.dot(x.astype(BF16), win_ref[...], preferred_element_type=F32)
    u = _gelu_exact(z[:, :A_WIDTH])
    v = _gelu_exact(z[:, A_WIDTH:2 * A_WIDTH])
    g_b = z[:, 2 * A_WIDTH:2 * A_WIDTH + B_WIDTH]
    g_c = z[:, 2 * A_WIDTH + B_WIDTH:2 * A_WIDTH + 2 * B_WIDTH]
    x_in = z[:, 2 * A_WIDTH + 2 * B_WIDTH:]

    vb = _layer_norm(v, ng_ref[...]).astype(BF16)
    first_head = lax.broadcasted_iota(I32, (A_CHUNK, LANES), 1) < A_HEAD_DIM
    zero = jnp.zeros((), BF16)
    chunks = []
    for c in range(tm // A_CHUNK):
        slabs = []
        for j in range(A_WIDTH // LANES):
            slab = vb[c * A_CHUNK:(c + 1) * A_CHUNK, j * LANES:(j + 1) * LANES]
            rhs = jnp.concatenate([jnp.where(first_head, slab, zero),
                                   jnp.where(first_head, zero, slab)], axis=0)
            slabs.append(jnp.dot(wcat_ref[j], rhs, preferred_element_type=F32))
        chunks.append(jnp.concatenate(slabs, axis=1) + bias_ref[...])
    y_a = u * jnp.concatenate(chunks, axis=0)

    zz = g_c * x_in

    @pl.when(i == 0)
    def _():
        zc_ref[0:CONV_HALO, :] = jnp.zeros((CONV_HALO, B_WIDTH), F32)

    zc_ref[CONV_HALO:CONV_HALO + tm, :] = zz
    z1 = zc_ref[CONV_HALO - 1:CONV_HALO - 1 + tm, :]
    z2 = zc_ref[CONV_HALO - 2:CONV_HALO - 2 + tm, :]
    cw = cw_ref[...]
    conv = cw[0:1, :] * z2 + cw[1:2, :] * z1 + cw[2:3, :] * zz
    y_b = g_b * conv
    zc_ref[0:CONV_HALO, :] = zc_ref[tm:tm + CONV_HALO, :]

    y = jnp.concatenate([y_a, y_b], axis=1).astype(BF16)
    mix = jnp.dot(y, wout_ref[...], preferred_element_type=F32)
    o_ref[...] = _layer_norm(DEEPNORM_ALPHA * x + mix, lg_ref[...], lb_ref[...])


def _odd_mixer_kernel(x_ref, win_ref, wgrp_ref, scale_ref, wout_ref, lg_ref, lb_ref,
                      o_ref, zh_ref):
    i = pl.program_id(0)
    tm = x_ref.shape[0]
    x = x_ref[...]
    z = jnp.dot(x.astype(BF16), win_ref[...], preferred_element_type=F32)

    @pl.when(i == 0)
    def _():
        zh_ref[0:POOL_HALO, :] = jnp.zeros((POOL_HALO, D_MODEL), F32)

    zh_ref[POOL_HALO:POOL_HALO + tm, :] = z
    pos = i * tm + lax.broadcasted_iota(I32, (tm, 1), 0)
    mixed = []
    for g, win in enumerate(C_WINDOWS):
        cols = slice(g * C_GROUP_DIM, (g + 1) * C_GROUP_DIM)
        acc = zh_ref[:, cols]
        shift = 1
        while shift < win:
            acc = acc + pltpu.roll(acc, shift, axis=0)
            shift *= 2
        cnt = jnp.minimum(pos + 1, win).astype(F32)
        pooled = acc[POOL_HALO:, :] / cnt - z[:, cols]
        mixed.append(jnp.dot(pooled.astype(BF16), wgrp_ref[g], preferred_element_type=F32))
    zh_ref[0:POOL_HALO, :] = zh_ref[tm:tm + POOL_HALO, :]
    y = (jnp.concatenate(mixed, axis=1) * scale_ref[...]).astype(BF16)
    mix = jnp.dot(y, wout_ref[...], preferred_element_type=F32)
    o_ref[...] = _layer_norm(DEEPNORM_ALPHA * x + mix, lg_ref[...], lb_ref[...])


def _full(shape):
    return pl.BlockSpec(shape, lambda i: (0,) * len(shape))


def _even_mixer(x, w_in, ng, wcat, bias, cw, w_out, lg, lb):
    t = x.shape[0]
    return pl.pallas_call(
        _even_mixer_kernel,
        grid=(t // MIX_TILE,),
        in_specs=[pl.BlockSpec((MIX_TILE, D_MODEL), lambda i: (i, 0)),
                  _full(w_in.shape), _full(ng.shape), _full(wcat.shape), _full(bias.shape),
                  _full(cw.shape), _full(w_out.shape), _full(lg.shape), _full(lb.shape)],
        out_specs=pl.BlockSpec((MIX_TILE, D_MODEL), lambda i: (i, 0)),
        out_shape=jax.ShapeDtypeStruct((t, D_MODEL), F32),
        scratch_shapes=[pltpu.VMEM((MIX_TILE + CONV_HALO, B_WIDTH), F32)],
        compiler_params=pltpu.CompilerParams(dimension_semantics=("arbitrary",),
                                             vmem_limit_bytes=VMEM_LIMIT),
        name="even_mixer",
    )(x, w_in, ng, wcat, bias, cw, w_out, lg, lb)


def _odd_mixer(x, w_in, w_grp, scale, w_out, lg, lb):
    t = x.shape[0]
    return pl.pallas_call(
        _odd_mixer_kernel,
        grid=(t // MIX_TILE,),
        in_specs=[pl.BlockSpec((MIX_TILE, D_MODEL), lambda i: (i, 0)),
                  _full(w_in.shape), _full(w_grp.shape), _full(scale.shape),
                  _full(w_out.shape), _full(lg.shape), _full(lb.shape)],
        out_specs=pl.BlockSpec((MIX_TILE, D_MODEL), lambda i: (i, 0)),
        out_shape=jax.ShapeDtypeStruct((t, D_MODEL), F32),
        scratch_shapes=[pltpu.VMEM((MIX_TILE + POOL_HALO, D_MODEL), F32)],
        compiler_params=pltpu.CompilerParams(dimension_semantics=("arbitrary",),
                                             vmem_limit_bytes=VMEM_LIMIT),
        name="odd_mixer",
    )(x, w_in, w_grp, scale, w_out, lg, lb)


def _route_kernel(x_ref, rwt_ref, rb_ref, upper_ref, lower_ref, ls_ref, g_ref, c8_ref):
    td = x_ref.shape[0]
    logits = lax.dot_general(rwt_ref[...], x_ref[...], (((1,), (1,)), ((), ())),
                             precision=lax.Precision.HIGHEST,
                             preferred_element_type=F32) + rb_ref[...]
    eio = lax.broadcasted_iota(I32, (N_EXPERTS, td), 0)
    work = logits
    picks, vals = [], []
    for _ in range(TOP_K):
        m = jnp.max(work, axis=0, keepdims=True)
        idx = jnp.min(jnp.where(work == m, eio, N_EXPERTS), axis=0, keepdims=True)
        pick = eio == idx
        picks.append(pick)
        vals.append(m)
        work = jnp.where(pick, -jnp.inf, work)
    ex = [jnp.exp(v - vals[0]) for v in vals]
    den = ex[0] + ex[1] + ex[2] + ex[3]
    gates = [e / den for e in ex]

    onehot = jnp.zeros((N_EXPERTS, td), F32)
    for pick in picks:
        onehot = onehot + pick.astype(F32)
    rank = jnp.dot(onehot.astype(BF16), upper_ref[...], preferred_element_type=F32)
    cnt = jnp.sum(onehot, axis=1, keepdims=True).astype(I32)
    c8 = ((cnt + (SUBLANES - 1)) // SUBLANES) * SUBLANES
    c8b = jnp.broadcast_to(c8.astype(F32), (N_EXPERTS, LANES))
    loff = jnp.dot(lower_ref[...], c8b.astype(BF16), preferred_element_type=F32)[:, 0:1]
    slot = rank + loff
    rows = [jnp.sum(jnp.where(pick, slot, 0.0), axis=0, keepdims=True) for pick in picks]
    pad = jnp.zeros((SUBLANES - TOP_K, td), F32)
    ls_ref[...] = jnp.concatenate(rows + [pad], axis=0).astype(I32)
    g_ref[...] = jnp.concatenate(gates + [pad], axis=0)
    c8_ref[...] = jnp.broadcast_to(c8, (N_EXPERTS, LANES))[None]


def _route(x1, rwt, rb, upper, lower):
    t = x1.shape[0]
    n_tiles = t // ROUTE_TILE
    return pl.pallas_call(
        _route_kernel,
        grid=(n_tiles,),
        in_specs=[pl.BlockSpec((ROUTE_TILE, D_MODEL), lambda i: (i, 0)),
                  _full(rwt.shape), _full(rb.shape), _full(upper.shape), _full(lower.shape)],
        out_specs=[pl.BlockSpec((SUBLANES, ROUTE_TILE), lambda i: (0, i)),
                   pl.BlockSpec((SUBLANES, ROUTE_TILE), lambda i: (0, i)),
                   pl.BlockSpec((1, N_EXPERTS, LANES), lambda i: (i, 0, 0))],
        out_shape=[jax.ShapeDtypeStruct((SUBLANES, t), I32),
                   jax.ShapeDtypeStruct((SUBLANES, t), F32),
                   jax.ShapeDtypeStruct((n_tiles, N_EXPERTS, LANES), I32)],
        compiler_params=pltpu.CompilerParams(dimension_semantics=("arbitrary",)),
        name="route",
    )(x1, rwt, rb, upper, lower)


def _slot_matches(ls_ref, k):
    s_iota = lax.broadcasted_iota(I32, (LOCAL_ROWS, ROUTE_TILE), 0)
    return s_iota == ls_ref[k:k + 1, :]


def _for_each_granule(tab_s, ngr_s, tile, fn):
    def body(g, c):
        fn(pl.multiple_of(g * SUBLANES, SUBLANES),
           pl.multiple_of(tab_s[tile * LOCAL_GRANULES + g], SUBLANES))
        return c
    lax.fori_loop(0, ngr_s[tile], body, 0)


def _wait_granules(n_granules, copy_of_rows):
    for bit in range(LOCAL_GRANULES.bit_length()):
        @pl.when((lax.shift_right_logical(n_granules, bit) & 1) == 1)
        def _():
            copy_of_rows(SUBLANES << bit).wait()


def _dispatch_kernel(tab_s, ngr_s, tst_s, tn_s,
                     x_ref, ls_ref, xpad_ref, xs_ref, zero_ref, sem, zsem):
    i = pl.program_id(0)
    n = pl.num_programs(0)
    slot = i % 2

    def rows_copy(buf, src_row, dst_row, rows=SUBLANES):
        return pltpu.make_async_copy(xs_ref.at[buf, pl.ds(src_row, rows)],
                                     xpad_ref.at[pl.ds(dst_row, rows)], sem.at[buf])

    def wait_tile(tile, buf):
        _wait_granules(ngr_s[tile], lambda rows: rows_copy(buf, 0, 0, rows))

    def zero_copy(dst_row):
        return pltpu.make_async_copy(zero_ref, xpad_ref.at[pl.ds(dst_row, SUBLANES)], zsem)

    @pl.when(i == 0)
    def _():
        zero_ref[...] = jnp.zeros(zero_ref.shape, F32)

        def per_expert(e, total):
            def per_granule(g, c):
                zero_copy(pl.multiple_of(tst_s[e] + g * SUBLANES, SUBLANES)).start()
                return c
            lax.fori_loop(0, tn_s[e], per_granule, 0)
            return total + tn_s[e]

        total = lax.fori_loop(0, N_EXPERTS, per_expert, 0)

        def drain(_, c):
            zero_copy(0).wait()
            return c
        lax.fori_loop(0, total, drain, 0)

    match = _slot_matches(ls_ref, 0)
    for k in range(1, TOP_K):
        match = match | _slot_matches(ls_ref, k)
    perm = jnp.where(match, 1.0, 0.0).astype(BF16)
    xs_ref[slot] = jnp.dot(perm, x_ref[...].astype(BF16), preferred_element_type=F32)

    _for_each_granule(tab_s, ngr_s, i, lambda lo, go: rows_copy(slot, lo, go).start())

    @pl.when(i > 0)
    def _():
        wait_tile(i - 1, 1 - slot)

    @pl.when(i == n - 1)
    def _():
        wait_tile(i, slot)


def _dispatch(tables, x1, lslot, n_rows):
    t = x1.shape[0]
    n_tiles = t // ROUTE_TILE
    grid_spec = pltpu.PrefetchScalarGridSpec(
        num_scalar_prefetch=len(tables),
        grid=(n_tiles,),
        in_specs=[pl.BlockSpec((ROUTE_TILE, D_MODEL), lambda i, *_: (i, 0)),
                  pl.BlockSpec((SUBLANES, ROUTE_TILE), lambda i, *_: (0, i))],
        out_specs=pl.BlockSpec(memory_space=pl.ANY),
        scratch_shapes=[pltpu.VMEM((2, LOCAL_ROWS, D_MODEL), F32),
                        pltpu.VMEM((SUBLANES, D_MODEL), F32),
                        pltpu.SemaphoreType.DMA((2,)),
                        pltpu.SemaphoreType.DMA],
    )
    return pl.pallas_call(
        _dispatch_kernel,
        grid_spec=grid_spec,
        out_shape=jax.ShapeDtypeStruct((n_rows, D_MODEL), F32),
        compiler_params=pltpu.CompilerParams(dimension_semantics=("arbitrary",),
                                             vmem_limit_bytes=VMEM_LIMIT),
        name="dispatch",
    )(*tables, x1, lslot)


def _expert_kernel(pstart_s, nblk_s, xpad_ref, wup_ref, bup_ref, wdn_ref, bdn_ref, ypad_ref,
                   wup_bf, wdn_bf, xbuf, ybuf, in_sem, out_sem):
    e = pl.program_id(0)
    nb = nblk_s[e]
    base = pstart_s[e]

    def block_rows(b):
        return pl.ds(pl.multiple_of(base + b * EXPERT_BLOCK, EXPERT_BLOCK), EXPERT_BLOCK)

    def in_copy(b, buf):
        return pltpu.make_async_copy(xpad_ref.at[block_rows(b)], xbuf.at[buf], in_sem.at[buf])

    def out_copy(b, buf):
        return pltpu.make_async_copy(ybuf.at[buf], ypad_ref.at[block_rows(b)], out_sem.at[buf])

    @pl.when(nb > 0)
    def _():
        in_copy(0, 0).start()
        wup_bf[...] = wup_ref[...].astype(BF16)
        wdn_bf[...] = wdn_ref[...].astype(BF16)

        def body(b, c):
            buf = b % 2

            @pl.when(b + 1 < nb)
            def _():
                in_copy(b + 1, 1 - buf).start()

            in_copy(b, buf).wait()

            @pl.when(b >= 2)
            def _():
                out_copy(b - 2, buf).wait()

            h = jnp.dot(xbuf[buf].astype(BF16), wup_bf[...],
                        preferred_element_type=F32) + bup_ref[...]
            gate = jnp.minimum(h[:, :D_FF], SWIGLU_LIMIT)
            lin = jnp.clip(h[:, D_FF:], -SWIGLU_LIMIT, SWIGLU_LIMIT)
            act = (lin + 1.0) * gate * jax.nn.sigmoid(SWIGLU_ALPHA * gate)
            ybuf[buf] = jnp.dot(act.astype(BF16), wdn_bf[...],
                                preferred_element_type=F32) + bdn_ref[...]
            out_copy(b, buf).start()
            return c

        lax.fori_loop(0, nb, body, 0)

        @pl.when(nb >= 2)
        def _():
            out_copy(nb - 2, nb % 2).wait()

        out_copy(nb - 1, (nb - 1) % 2).wait()


def _experts(layer, pstart, nblk, x_pad, w_up, b_up, w_down, b_down):
    def w_map(e, *_):
        return (layer, e, 0, 0)

    grid_spec = pltpu.PrefetchScalarGridSpec(
        num_scalar_prefetch=2,
        grid=(N_EXPERTS,),
        in_specs=[pl.BlockSpec(memory_space=pl.ANY),
                  pl.BlockSpec((None, None, D_MODEL, 2 * D_FF), w_map),
                  pl.BlockSpec((None, None, 1, 2 * D_FF), w_map),
                  pl.BlockSpec((None, None, D_FF, D_MODEL), w_map),
                  pl.BlockSpec((None, None, 1, D_MODEL), w_map)],
        out_specs=pl.BlockSpec(memory_space=pl.ANY),
        scratch_shapes=[pltpu.VMEM((D_MODEL, 2 * D_FF), BF16),
                        pltpu.VMEM((D_FF, D_MODEL), BF16),
                        pltpu.VMEM((2, EXPERT_BLOCK, D_MODEL), F32),
                        pltpu.VMEM((2, EXPERT_BLOCK, D_MODEL), F32),
                        pltpu.SemaphoreType.DMA((2,)),
                        pltpu.SemaphoreType.DMA((2,))],
    )
    return pl.pallas_call(
        _expert_kernel,
        grid_spec=grid_spec,
        out_shape=jax.ShapeDtypeStruct(x_pad.shape, F32),
        compiler_params=pltpu.CompilerParams(dimension_semantics=("arbitrary",),
                                             vmem_limit_bytes=VMEM_LIMIT),
        name="experts",
    )(pstart, nblk, x_pad, w_up, b_up, w_down, b_down)


def _combine_kernel(tab_s, ngr_s,
                    x_ref, ls_ref, g_ref, lg_ref, lb_ref, ypad_ref, o_ref, yl_ref, sem):
    i = pl.program_id(0)
    n = pl.num_programs(0)
    slot = i % 2

    def rows_copy(buf, src_row, dst_row, rows=SUBLANES):
        return pltpu.make_async_copy(ypad_ref.at[pl.ds(src_row, rows)],
                                     yl_ref.at[buf, pl.ds(dst_row, rows)], sem.at[buf])

    def fetch(tile, buf):
        _for_each_granule(tab_s, ngr_s, tile, lambda lo, go: rows_copy(buf, go, lo).start())

    @pl.when(i == 0)
    def _():
        yl_ref[...] = jnp.zeros(yl_ref.shape, F32)
        fetch(0, 0)

    @pl.when(i + 1 < n)
    def _():
        fetch(i + 1, 1 - slot)

    _wait_granules(ngr_s[i], lambda rows: rows_copy(slot, 0, 0, rows))

    gated =jnp.zeros((LOCAL_ROWS, ROUTE_TILE), F32)
    for k in range(TOP_K):
        gated = gated + jnp.where(_slot_matches(ls_ref, k), g_ref[k:k + 1, :], 0.0)
    ffn = jnp.dot(gated.T.astype(BF16), yl_ref[slot].astype(BF16),
                  preferred_element_type=F32)
    o_ref[...] = _layer_norm(DEEPNORM_ALPHA * x_ref[...] + ffn, lg_ref[...], lb_ref[...])


def _combine(tables, x1, lslot, gates, lg, lb, y_pad):
    t = x1.shape[0]
    n_tiles = t // ROUTE_TILE
    grid_spec = pltpu.PrefetchScalarGridSpec(
        num_scalar_prefetch=len(tables),
        grid=(n_tiles,),
        in_specs=[pl.BlockSpec((ROUTE_TILE, D_MODEL), lambda i, *_: (i, 0)),
                  pl.BlockSpec((SUBLANES, ROUTE_TILE), lambda i, *_: (0, i)),
                  pl.BlockSpec((SUBLANES, ROUTE_TILE), lambda i, *_: (0, i)),
                  pl.BlockSpec(lg.shape, lambda i, *_: (0, 0)),
                  pl.BlockSpec(lb.shape, lambda i, *_: (0, 0)),
                  pl.BlockSpec(memory_space=pl.ANY)],
        out_specs=pl.BlockSpec((ROUTE_TILE, D_MODEL), lambda i, *_: (i, 0)),
        scratch_shapes=[pltpu.VMEM((2, LOCAL_ROWS, D_MODEL), F32),
                        pltpu.SemaphoreType.DMA((2,))],
    )
    return pl.pallas_call(
        _combine_kernel,
        grid_spec=grid_spec,
        out_shape=jax.ShapeDtypeStruct((t, D_MODEL), F32),
        compiler_params=pltpu.CompilerParams(dimension_semantics=("arbitrary",),
                                             vmem_limit_bytes=VMEM_LIMIT),
        name="combine",
    )(*tables, x1, lslot, gates, lg, lb, y_pad)


def _grouped_rows_bound(n_tokens):
    n_tiles = n_tokens // ROUTE_TILE
    worst = (n_tokens * TOP_K + n_tiles * N_EXPERTS * (SUBLANES - 1)
             + N_EXPERTS * (EXPERT_BLOCK - 1))
    return -(-worst // EXPERT_BLOCK) * EXPERT_BLOCK


def _routing_tables(c8):
    seg_total = jnp.sum(c8, axis=0)
    padded = (seg_total + EXPERT_BLOCK - 1) // EXPERT_BLOCK * EXPERT_BLOCK
    pstart = jnp.cumsum(padded) - padded
    gstart = pstart[None, :] + jnp.cumsum(c8, axis=0) - c8
    lend = jnp.cumsum(c8, axis=1)
    g_row = jnp.arange(LOCAL_GRANULES, dtype=I32) * SUBLANES
    owner = jnp.sum((lend[:, None, :] <= g_row[None, :, None]).astype(I32), axis=2)
    owner = jnp.minimum(owner, N_EXPERTS - 1)
    shift = jnp.take_along_axis(gstart - (lend - c8), owner, axis=1)
    granule_dst = shift + g_row[None, :]
    ngran = lend[:, -1] // SUBLANES
    tail_start = pstart + seg_total
    tail_n = (padded - seg_total) // SUBLANES
    as_i32 = lambda a: a.astype(I32)
    return ((as_i32(granule_dst.reshape(-1)), as_i32(ngran)),
            (as_i32(tail_start), as_i32(tail_n)),
            (as_i32(pstart), as_i32(padded // EXPERT_BLOCK)))


def _moe(layer, x1, rwt, rb, upper, lower, w_up, b_up, w_down, b_down, lg, lb):
    t = x1.shape[0]
    lslot, gates, c8 = _route(x1, rwt, rb, upper, lower)
    tile_tables, tail_tables, expert_tables = _routing_tables(c8[:, :, 0])
    x_pad = _dispatch(tile_tables + tail_tables, x1, lslot, _grouped_rows_bound(t))
    y_pad = _experts(layer, *expert_tables, x_pad, w_up, b_up, w_down, b_down)
    return _combine(tile_tables, x1, lslot, gates, lg, lb, y_pad)


def kernel(x, ab_w_in, a_norm_g, a_w_s, a_b_s, b_conv_w, ab_w_out, c_w_in, c_w_grp, c_scale,
           c_w_out, ln_mix_g, ln_mix_b, router_w, router_b, moe_w_up, moe_b_up, moe_w_down,
           moe_b_down, ln_ffn_g, ln_ffn_b):
    bsz, seq, d = x.shape
    assert d == D_MODEL and (bsz * seq) % MIX_TILE == 0 and seq % A_CHUNK == 0
    assert bsz == 1, "token tiles carry conv / pooling state across the sequence"
    h = x.reshape(bsz * seq, d)

    upper = jnp.triu(jnp.ones((ROUTE_TILE, ROUTE_TILE), BF16), k=1)
    lower = jnp.tril(jnp.ones((N_EXPERTS, N_EXPERTS), BF16), k=-1)
    causal = jnp.tril(jnp.ones((A_CHUNK, A_CHUNK), dtype=bool))
    b_up = moe_b_up.reshape(DEPTH, N_EXPERTS, 1, 2 * D_FF)
    b_down = moe_b_down.reshape(DEPTH, N_EXPERTS, 1, D_MODEL)

    for layer in range(DEPTH):
        i = layer // 2
        lg = ln_mix_g[layer].reshape(1, d)
        lb = ln_mix_b[layer].reshape(1, d)
        if layer % 2 == 0:
            w_s = jnp.where(causal, a_w_s[i], 0.0).astype(BF16)
            wcat = w_s.reshape(A_HEADS // 2, 2, A_CHUNK, A_CHUNK).transpose(0, 2, 1, 3)
            wcat = wcat.reshape(A_HEADS // 2, A_CHUNK, 2 * A_CHUNK)
            bias = jnp.repeat(a_b_s[i].T, A_HEAD_DIM, axis=1)
            h = _even_mixer(h, ab_w_in[i].astype(BF16), a_norm_g[i].reshape(1, A_WIDTH),
                            wcat, bias, b_conv_w[i], ab_w_out[i].astype(BF16), lg, lb)
        else:
            h = _odd_mixer(h, c_w_in[i].astype(BF16), c_w_grp[i].astype(BF16),
                           c_scale[i].reshape(1, d), c_w_out[i].astype(BF16), lg, lb)
        h = _moe(layer, h, router_w[layer].T, router_b[layer].reshape(N_EXPERTS, 1),
                 upper, lower, moe_w_up, b_up, moe_w_down, b_down,
                 ln_ffn_g[layer].reshape(1, d), ln_ffn_b[layer].reshape(1, d))
    return h.reshape(bsz, seq, d)
```

```python
import functools
import math

import jax
import jax.numpy as jnp
from jax import lax
from jax.experimental import pallas as pl
from jax.experimental.pallas import tpu as pltpu

F32 = jnp.float32
BF16 = jnp.bfloat16
I32 = jnp.int32

D_MODEL = 1024
DEPTH = 4
A_WIDTH = 512
A_HEADS = 8
A_HEAD_DIM = 64
A_CHUNK = 128
B_WIDTH = 512
PROJ_AB = 2 * A_WIDTH + 3 * B_WIDTH
C_GROUPS = 4
C_WINDOWS = (2, 4, 8, 16)
C_GROUP_DIM = 256
N_EXPERTS = 32
TOP_K = 4
D_FF = 1024
SWIGLU_LIMIT = 7.0
SWIGLU_ALPHA = 1.702
LN_EPS = 1e-5
DEEPNORM_ALPHA = float((2 * DEPTH) ** 0.25)

SUBLANES = 8
LANES = 128
MIX_TILE = 512
CONV_HALO = 8
POOL_HALO = 16
ROUTE_TILE = 256
EXPERT_BLOCK = 256
LOCAL_ROWS = ROUTE_TILE * TOP_K + N_EXPERTS * SUBLANES
LOCAL_GRANULES = LOCAL_ROWS // SUBLANES
VMEM_LIMIT = 56 * 1024 * 1024
BLOCK_DMA_PRIORITY = 1


def _layer_norm(x, g, b=None):
    mu = jnp.mean(x, axis=-1, keepdims=True)
    xc = x - mu
    var = jnp.mean(xc * xc, axis=-1, keepdims=True)
    y = xc * lax.rsqrt(var + LN_EPS) * g
    if b is not None:
        y = y + b
    return y


def _gelu_exact(x):
    return 0.5 * x * (1.0 + lax.erf(x * (1.0 / math.sqrt(2.0))))


def _even_mixer_kernel(x_ref, win_ref, ng_ref, wcat_ref, bias_ref, cw_ref, wout_ref,
                       lg_ref, lb_ref, o_ref, zc_ref):
    i = pl.program_id(0)
    tm = x_ref.shape[0]
    x = x_ref[...]
    z = jnp.dot(x.astype(BF16), win_ref[...], preferred_element_type=F32)
    u = _gelu_exact(z[:, :A_WIDTH])
    v = _gelu_exact(z[:, A_WIDTH:2 * A_WIDTH])
    g_b = z[:, 2 * A_WIDTH:2 * A_WIDTH + B_WIDTH]
    g_c = z[:, 2 * A_WIDTH + B_WIDTH:2 * A_WIDTH + 2 * B_WIDTH]
    x_in = z[:, 2 * A_WIDTH + 2 * B_WIDTH:]

    vb = _layer_norm(v, ng_ref[...]).astype(BF16)
    first_head = lax.broadcasted_iota(I32, (A_CHUNK, LANES), 1) < A_HEAD_DIM
    zero = jnp.zeros((), BF16)
    chunks = []
    for c in range(tm // A_CHUNK):
        slabs = []
        for j in range(A_WIDTH // LANES):
            slab = vb[c * A_CHUNK:(c + 1) * A_CHUNK, j * LANES:(j + 1) * LANES]
            rhs = jnp.concatenate([jnp.where(first_head, slab, zero),
                                   jnp.where(first_head, zero, slab)], axis=0)
            slabs.append(jnp.dot(wcat_ref[j], rhs, preferred_element_type=F32))
        chunks.append(jnp.concatenate(slabs, axis=1) + bias_ref[...])
    y_a = u * jnp.concatenate(chunks, axis=0)

    zz = g_c * x_in

    @pl.when(i == 0)
    def _():
        zc_ref[0:CONV_HALO, :] = jnp.zeros((CONV_HALO, B_WIDTH), F32)

    zc_ref[CONV_HALO:CONV_HALO + tm, :] = zz
    z1 = zc_ref[CONV_HALO - 1:CONV_HALO - 1 + tm, :]
    z2 = zc_ref[CONV_HALO - 2:CONV_HALO - 2 + tm, :]
    cw = cw_ref[...]
    conv = cw[0:1, :] * z2 + cw[1:2, :] * z1 + cw[2:3, :] * zz
    y_b = g_b * conv
    zc_ref[0:CONV_HALO, :] = zc_ref[tm:tm + CONV_HALO, :]

    y = jnp.concatenate([y_a, y_b], axis=1).astype(BF16)
    mix = jnp.dot(y, wout_ref[...], preferred_element_type=F32)
    o_ref[...] = _layer_norm(DEEPNORM_ALPHA * x + mix, lg_ref[...], lb_ref[...])


def _odd_mixer_kernel(x_ref, win_ref, wgrp_ref, scale_ref, wout_ref, lg_ref, lb_ref,
                      o_ref, zh_ref):
    i = pl.program_id(0)
    tm = x_ref.shape[0]
    x = x_ref[...]
    z = jnp.dot(x.astype(BF16), win_ref[...], preferred_element_type=F32)

    @pl.when(i == 0)
    def _():
        zh_ref[0:POOL_HALO, :] = jnp.zeros((POOL_HALO, D_MODEL), F32)

    zh_ref[POOL_HALO:POOL_HALO + tm, :] = z
    pos = i * tm + lax.broadcasted_iota(I32, (tm, 1), 0)
    mixed = []
    for g, win in enumerate(C_WINDOWS):
        cols = slice(g * C_GROUP_DIM, (g + 1) * C_GROUP_DIM)
        acc = zh_ref[:, cols]
        shift = 1
        while shift < win:
            acc = acc + pltpu.roll(acc, shift, axis=0)
            shift *= 2
        cnt = jnp.minimum(pos + 1, win).astype(F32)
        pooled = acc[POOL_HALO:, :] / cnt - z[:, cols]
        mixed.append(jnp.dot(pooled.astype(BF16), wgrp_ref[g], preferred_element_type=F32))
    zh_ref[0:POOL_HALO, :] = zh_ref[tm:tm + POOL_HALO, :]
    y = (jnp.concatenate(mixed, axis=1) * scale_ref[...]).astype(BF16)
    mix = jnp.dot(y, wout_ref[...], preferred_element_type=F32)
    o_ref[...] = _layer_norm(DEEPNORM_ALPHA * x + mix, lg_ref[...], lb_ref[...])


def _full(shape):
    return pl.BlockSpec(shape, lambda i: (0,) * len(shape))


def _even_mixer(x, w_in, ng, wcat, bias, cw, w_out, lg, lb):
    t = x.shape[0]
    return pl.pallas_call(
        _even_mixer_kernel,
        grid=(t // MIX_TILE,),
        in_specs=[pl.BlockSpec((MIX_TILE, D_MODEL), lambda i: (i, 0)),
                  _full(w_in.shape), _full(ng.shape), _full(wcat.shape), _full(bias.shape),
                  _full(cw.shape), _full(w_out.shape), _full(lg.shape), _full(lb.shape)],
        out_specs=pl.BlockSpec((MIX_TILE, D_MODEL), lambda i: (i, 0)),
        out_shape=jax.ShapeDtypeStruct((t, D_MODEL), F32),
        scratch_shapes=[pltpu.VMEM((MIX_TILE + CONV_HALO, B_WIDTH), F32)],
        compiler_params=pltpu.CompilerParams(dimension_semantics=("arbitrary",),
                                             vmem_limit_bytes=VMEM_LIMIT),
        name="even_mixer",
    )(x, w_in, ng, wcat, bias, cw, w_out, lg, lb)


def _odd_mixer(x, w_in, w_grp, scale, w_out, lg, lb):
    t = x.shape[0]
    return pl.pallas_call(
        _odd_mixer_kernel,
        grid=(t // MIX_TILE,),
        in_specs=[pl.BlockSpec((MIX_TILE, D_MODEL), lambda i: (i, 0)),
                  _full(w_in.shape), _full(w_grp.shape), _full(scale.shape),
                  _full(w_out.shape), _full(lg.shape), _full(lb.shape)],
        out_specs=pl.BlockSpec((MIX_TILE, D_MODEL), lambda i: (i, 0)),
        out_shape=jax.ShapeDtypeStruct((t, D_MODEL), F32),
        scratch_shapes=[pltpu.VMEM((MIX_TILE + POOL_HALO, D_MODEL), F32)],
        compiler_params=pltpu.CompilerParams(dimension_semantics=("arbitrary",),
                                             vmem_limit_bytes=VMEM_LIMIT),
        name="odd_mixer",
    )(x, w_in, w_grp, scale, w_out, lg, lb)


def _route_kernel(x_ref, rw_ref, rb_ref, upper_ref, lower_ref, ls_ref, g_ref, c8_ref):
    td = x_ref.shape[0]
    x = x_ref[...]
    x_hi = x.astype(BF16)
    x_lo = (x - x_hi.astype(F32)).astype(BF16)
    by_hi = jnp.dot(x_hi, rw_ref[...], preferred_element_type=F32)
    by_lo = jnp.dot(x_lo, rw_ref[:, :LANES], preferred_element_type=F32)
    token_major = by_hi[:, :LANES] + (by_hi[:, LANES:] + by_lo)
    logits = token_major.T[:N_EXPERTS, :] + rb_ref[...]
    eio = lax.broadcasted_iota(I32, (N_EXPERTS, td), 0)
    work = logits
    picks, vals = [], []
    for _ in range(TOP_K):
        m = jnp.max(work, axis=0, keepdims=True)
        idx = jnp.min(jnp.where(work == m, eio, N_EXPERTS), axis=0, keepdims=True)
        pick = eio == idx
        picks.append(pick)
        vals.append(m)
        work = jnp.where(pick, -jnp.inf, work)
    ex = [jnp.exp(v - vals[0]) for v in vals]
    den = ex[0] + ex[1] + ex[2] + ex[3]
    gates = [e / den for e in ex]

    onehot = jnp.zeros((N_EXPERTS, td), F32)
    for pick in picks:
        onehot = onehot + pick.astype(F32)
    rank = jnp.dot(onehot.astype(BF16), upper_ref[...], preferred_element_type=F32)
    cnt = jnp.sum(onehot, axis=1, keepdims=True).astype(I32)
    c8 = ((cnt + (SUBLANES - 1)) // SUBLANES) * SUBLANES
    c8b = jnp.broadcast_to(c8.astype(F32), (N_EXPERTS, LANES))
    loff = jnp.dot(lower_ref[...], c8b.astype(BF16), preferred_element_type=F32)[:, 0:1]
    slot = rank + loff
    rows = [jnp.sum(jnp.where(pick, slot, 0.0), axis=0, keepdims=True) for pick in picks]
    pad = jnp.zeros((SUBLANES - TOP_K, td), F32)
    ls_ref[...] = jnp.concatenate(rows + [pad], axis=0).astype(I32)
    g_ref[...] = jnp.concatenate(gates + [pad], axis=0)
    c8_ref[...] = jnp.broadcast_to(c8, (N_EXPERTS, LANES))[None]


def _split_router_weight(router_w):
    hi = router_w.astype(BF16)
    lo = (router_w - hi.astype(F32)).astype(BF16)
    pad = ((0, 0), (0, LANES - N_EXPERTS))
    return jnp.concatenate([jnp.pad(hi, pad), jnp.pad(lo, pad)], axis=1)


def _route(x1, rwt, rb, upper, lower):
    t = x1.shape[0]
    n_tiles = t // ROUTE_TILE
    return pl.pallas_call(
        _route_kernel,
        grid=(n_tiles,),
        in_specs=[pl.BlockSpec((ROUTE_TILE, D_MODEL), lambda i: (i, 0)),
                  _full(rwt.shape), _full(rb.shape), _full(upper.shape), _full(lower.shape)],
        out_specs=[pl.BlockSpec((SUBLANES, ROUTE_TILE), lambda i: (0, i)),
                   pl.BlockSpec((SUBLANES, ROUTE_TILE), lambda i: (0, i)),
                   pl.BlockSpec((1, N_EXPERTS, LANES), lambda i: (i, 0, 0))],
        out_shape=[jax.ShapeDtypeStruct((SUBLANES, t), I32),
                   jax.ShapeDtypeStruct((SUBLANES, t), F32),
                   jax.ShapeDtypeStruct((n_tiles, N_EXPERTS, LANES), I32)],
        compiler_params=pltpu.CompilerParams(dimension_semantics=("arbitrary",)),
        name="route",
    )(x1, rwt, rb, upper, lower)


def _slot_matches(ls_ref, k):
    s_iota = lax.broadcasted_iota(I32, (LOCAL_ROWS, ROUTE_TILE), 0)
    return s_iota == ls_ref[k:k + 1, :]


def _for_each_granule(tab_s, ngr_s, tile, fn):
    def body(g, c):
        fn(pl.multiple_of(g * SUBLANES, SUBLANES),
           pl.multiple_of(tab_s[tile * LOCAL_GRANULES + g], SUBLANES))
        return c
    lax.fori_loop(0, ngr_s[tile], body, 0)


def _wait_granules(n_granules, copy_of_rows):
    for bit in range(LOCAL_GRANULES.bit_length()):
        @pl.when((lax.shift_right_logical(n_granules, bit) & 1) == 1)
        def _():
            copy_of_rows(SUBLANES << bit).wait()


def _dispatch_kernel(tab_s, ngr_s, tst_s, tn_s,
                     x_ref, ls_ref, xpad_ref, xs_ref, zero_ref, sem, zsem):
    i = pl.program_id(0)
    n = pl.num_programs(0)
    slot = i % 2

    def rows_copy(buf, src_row, dst_row, rows=SUBLANES):
        return pltpu.make_async_copy(xs_ref.at[buf, pl.ds(src_row, rows)],
                                     xpad_ref.at[pl.ds(dst_row, rows)], sem.at[buf])

    def wait_tile(tile, buf):
        _wait_granules(ngr_s[tile], lambda rows: rows_copy(buf, 0, 0, rows))

    def zero_copy(dst_row):
        return pltpu.make_async_copy(zero_ref, xpad_ref.at[pl.ds(dst_row, SUBLANES)], zsem)

    @pl.when(i == 0)
    def _():
        zero_ref[...] = jnp.zeros(zero_ref.shape, F32)

        def per_expert(e, total):
            def per_granule(g, c):
                zero_copy(pl.multiple_of(tst_s[e] + g * SUBLANES, SUBLANES)).start()
                return c
            lax.fori_loop(0, tn_s[e], per_granule, 0)
            return total + tn_s[e]

        total = lax.fori_loop(0, N_EXPERTS, per_expert, 0)

        def drain(_, c):
            zero_copy(0).wait()
            return c
        lax.fori_loop(0, total, drain, 0)

    match = _slot_matches(ls_ref, 0)
    for k in range(1, TOP_K):
        match = match | _slot_matches(ls_ref, k)
    perm = jnp.where(match, 1.0, 0.0).astype(BF16)
    xs_ref[slot] = jnp.dot(perm, x_ref[...].astype(BF16), preferred_element_type=F32)

    _for_each_granule(tab_s, ngr_s, i, lambda lo, go: rows_copy(slot, lo, go).start())

    @pl.when(i > 0)
    def _():
        wait_tile(i - 1, 1 - slot)

    @pl.when(i == n - 1)
    def _():
        wait_tile(i, slot)


def _dispatch(tables, x1, lslot, n_rows):
    t = x1.shape[0]
    n_tiles = t // ROUTE_TILE
    grid_spec = pltpu.PrefetchScalarGridSpec(
        num_scalar_prefetch=len(tables),
        grid=(n_tiles,),
        in_specs=[pl.BlockSpec((ROUTE_TILE, D_MODEL), lambda i, *_: (i, 0)),
                  pl.BlockSpec((SUBLANES, ROUTE_TILE), lambda i, *_: (0, i))],
        out_specs=pl.BlockSpec(memory_space=pl.ANY),
        scratch_shapes=[pltpu.VMEM((2, LOCAL_ROWS, D_MODEL), F32),
                        pltpu.VMEM((SUBLANES, D_MODEL), F32),
                        pltpu.SemaphoreType.DMA((2,)),
                        pltpu.SemaphoreType.DMA],
    )
    return pl.pallas_call(
        _dispatch_kernel,
        grid_spec=grid_spec,
        out_shape=jax.ShapeDtypeStruct((n_rows, D_MODEL), F32),
        compiler_params=pltpu.CompilerParams(dimension_semantics=("arbitrary",),
                                             vmem_limit_bytes=VMEM_LIMIT),
        name="dispatch",
    )(*tables, x1, lslot)


def _expert_kernel(pstart_s, nblk_s, xpad_ref, wup_ref, bup_ref, wdn_ref, bdn_ref, ypad_ref,
                   wup_bf, wdn_bf, xbuf, ybuf, in_sem, out_sem):
    e = pl.program_id(0)
    nb = nblk_s[e]
    base = pstart_s[e]

    def block_rows(b):
        return pl.ds(pl.multiple_of(base + b * EXPERT_BLOCK, EXPERT_BLOCK), EXPERT_BLOCK)

    def in_copy(b, buf):
        return pltpu.make_async_copy(xpad_ref.at[block_rows(b)], xbuf.at[buf], in_sem.at[buf])

    def out_copy(b, buf):
        return pltpu.make_async_copy(ybuf.at[buf], ypad_ref.at[block_rows(b)], out_sem.at[buf])

    @pl.when(nb > 0)
    def _():
        in_copy(0, 0).start(priority=BLOCK_DMA_PRIORITY)
        wup_bf[...] = wup_ref[...].astype(BF16)
        wdn_bf[...] = wdn_ref[...].astype(BF16)

        def body(b, c):
            buf = b % 2

            @pl.when(b + 1 < nb)
            def _():
                in_copy(b + 1, 1 - buf).start(priority=BLOCK_DMA_PRIORITY)

            in_copy(b, buf).wait()

            @pl.when(b >= 2)
            def _():
                out_copy(b - 2, buf).wait()

            h = jnp.dot(xbuf[buf].astype(BF16), wup_bf[...],
                        preferred_element_type=F32) + bup_ref[...]
            gate = jnp.minimum(h[:, :D_FF], SWIGLU_LIMIT)
            lin = jnp.clip(h[:, D_FF:], -SWIGLU_LIMIT, SWIGLU_LIMIT)
            act = (lin + 1.0) * gate * jax.nn.sigmoid(SWIGLU_ALPHA * gate)
            ybuf[buf] = jnp.dot(act.astype(BF16), wdn_bf[...],
                                preferred_element_type=F32) + bdn_ref[...]
            out_copy(b, buf).start(priority=BLOCK_DMA_PRIORITY)
            return c

        lax.fori_loop(0, nb, body, 0)

        @pl.when(nb >= 2)
        def _():
            out_copy(nb - 2, nb % 2).wait()

        out_copy(nb - 1, (nb - 1) % 2).wait()


def _experts(layer, pstart, nblk, x_pad, w_up, b_up, w_down, b_down):
    def w_map(e, *_):
        return (layer, e, 0, 0)

    grid_spec = pltpu.PrefetchScalarGridSpec(
        num_scalar_prefetch=2,
        grid=(N_EXPERTS,),
        in_specs=[pl.BlockSpec(memory_space=pl.ANY),
                  pl.BlockSpec((None, None, D_MODEL, 2 * D_FF), w_map),
                  pl.BlockSpec((None, None, 1, 2 * D_FF), w_map),
                  pl.BlockSpec((None, None, D_FF, D_MODEL), w_map),
                  pl.BlockSpec((None, None, 1, D_MODEL), w_map)],
        out_specs=pl.BlockSpec(memory_space=pl.ANY),
        scratch_shapes=[pltpu.VMEM((D_MODEL, 2 * D_FF), BF16),
                        pltpu.VMEM((D_FF, D_MODEL), BF16),
                        pltpu.VMEM((2, EXPERT_BLOCK, D_MODEL), F32),
                        pltpu.VMEM((2, EXPERT_BLOCK, D_MODEL), F32),
                        pltpu.SemaphoreType.DMA((2,)),
                        pltpu.SemaphoreType.DMA((2,))],
    )
    return pl.pallas_call(
        _expert_kernel,
        grid_spec=grid_spec,
        out_shape=jax.ShapeDtypeStruct(x_pad.shape, F32),
        compiler_params=pltpu.CompilerParams(dimension_semantics=("arbitrary",),
                                             vmem_limit_bytes=VMEM_LIMIT),
        name="experts",
    )(pstart, nblk, x_pad, w_up, b_up, w_down, b_down)


def _combine_kernel(tab_s, ngr_s,
                    x_ref, ls_ref, g_ref, lg_ref, lb_ref, ypad_ref, o_ref, yl_ref, sem):
    i = pl.program_id(0)
    n = pl.num_programs(0)
    slot = i % 2

    def rows_copy(buf, src_row, dst_row, rows=SUBLANES):
        return pltpu.make_async_copy(ypad_ref.at[pl.ds(src_row, rows)],
                                     yl_ref.at[buf, pl.ds(dst_row, rows)], sem.at[buf])

    def fetch(tile, buf):
        _for_each_granule(tab_s, ngr_s, tile, lambda lo, go: rows_copy(buf, go, lo).start())

    @pl.when(i == 0)
    def _():
        yl_ref[...] = jnp.zeros(yl_ref.shape, F32)
        fetch(0, 0)

    @pl.when(i + 1 < n)
    def _():
        fetch(i + 1, 1 - slot)

    _wait_granules(ngr_s[i], lambda rows: rows_copy(slot, 0, 0, rows))

    gated = jnp.zeros((LOCAL_ROWS, ROUTE_TILE), F32)
    for k in range(TOP_K):
        gated = jnp.where(_slot_matches(ls_ref, k), g_ref[k:k + 1, :], gated)
    ffn = jnp.dot(gated.T.astype(BF16), yl_ref[slot].astype(BF16),
                  preferred_element_type=F32)
    o_ref[...] = _layer_norm(DEEPNORM_ALPHA * x_ref[...] + ffn, lg_ref[...], lb_ref[...])


def _combine(tables, x1, lslot, gates, lg, lb, y_pad):
    t = x1.shape[0]
    n_tiles = t // ROUTE_TILE
    grid_spec = pltpu.PrefetchScalarGridSpec(
        num_scalar_prefetch=len(tables),
        grid=(n_tiles,),
        in_specs=[pl.BlockSpec((ROUTE_TILE, D_MODEL), lambda i, *_: (i, 0)),
                  pl.BlockSpec((SUBLANES, ROUTE_TILE), lambda i, *_: (0, i)),
                  pl.BlockSpec((SUBLANES, ROUTE_TILE), lambda i, *_: (0, i)),
                  pl.BlockSpec(lg.shape, lambda i, *_: (0, 0)),
                  pl.BlockSpec(lb.shape, lambda i, *_: (0, 0)),
                  pl.BlockSpec(memory_space=pl.ANY)],
        out_specs=pl.BlockSpec((ROUTE_TILE, D_MODEL), lambda i, *_: (i, 0)),
        scratch_shapes=[pltpu.VMEM((2, LOCAL_ROWS, D_MODEL), F32),
                        pltpu.SemaphoreType.DMA((2,))],
    )
    return pl.pallas_call(
        _combine_kernel,
        grid_spec=grid_spec,
        out_shape=jax.ShapeDtypeStruct((t, D_MODEL), F32),
        compiler_params=pltpu.CompilerParams(dimension_semantics=("arbitrary",),
                                             vmem_limit_bytes=VMEM_LIMIT),
        name="combine",
    )(*tables, x1, lslot, gates, lg, lb, y_pad)


def _grouped_rows_bound(n_tokens):
    n_tiles = n_tokens // ROUTE_TILE
    worst = (n_tokens * TOP_K + n_tiles * N_EXPERTS * (SUBLANES - 1)
             + N_EXPERTS * (EXPERT_BLOCK - 1))
    return -(-worst // EXPERT_BLOCK) * EXPERT_BLOCK


def _routing_tables(c8):
    seg_total = jnp.sum(c8, axis=0)
    padded = (seg_total + EXPERT_BLOCK - 1) // EXPERT_BLOCK * EXPERT_BLOCK
    pstart = jnp.cumsum(padded) - padded
    gstart = pstart[None, :] + jnp.cumsum(c8, axis=0) - c8
    lend = jnp.cumsum(c8, axis=1)
    g_row = jnp.arange(LOCAL_GRANULES, dtype=I32) * SUBLANES
    loff = lend - c8
    row = g_row[None, :, None]
    inside = (loff[:, None, :] <= row) & (row < lend[:, None, :])
    shift = jnp.sum(jnp.where(inside, (gstart - loff)[:, None, :], 0), axis=2)
    granule_dst = shift + g_row[None, :]
    ngran = lend[:, -1] // SUBLANES
    tail_start = pstart + seg_total
    tail_n = (padded - seg_total) // SUBLANES
    as_i32 = lambda a: a.astype(I32)
    return ((as_i32(granule_dst.reshape(-1)), as_i32(ngran)),
            (as_i32(tail_start), as_i32(tail_n)),
            (as_i32(pstart), as_i32(padded // EXPERT_BLOCK)))


def _moe(layer, x1, rwt, rb, upper, lower, w_up, b_up, w_down, b_down, lg, lb):
    t = x1.shape[0]
    lslot, gates, c8 = _route(x1, rwt, rb, upper, lower)
    tile_tables, tail_tables, expert_tables = _routing_tables(c8[:, :, 0])
    x_pad = _dispatch(tile_tables + tail_tables, x1, lslot, _grouped_rows_bound(t))
    y_pad = _experts(layer, *expert_tables, x_pad, w_up, b_up, w_down, b_down)
    return _combine(tile_tables, x1, lslot, gates, lg, lb, y_pad)


def kernel(x, ab_w_in, a_norm_g, a_w_s, a_b_s, b_conv_w, ab_w_out, c_w_in, c_w_grp, c_scale,
           c_w_out, ln_mix_g, ln_mix_b, router_w, router_b, moe_w_up, moe_b_up, moe_w_down,
           moe_b_down, ln_ffn_g, ln_ffn_b):
    bsz, seq, d = x.shape
    assert d == D_MODEL and (bsz * seq) % MIX_TILE == 0 and seq % A_CHUNK == 0
    assert bsz == 1, "token tiles carry conv / pooling state across the sequence"
    h = x.reshape(bsz * seq, d)

    upper = jnp.triu(jnp.ones((ROUTE_TILE, ROUTE_TILE), BF16), k=1)
    lower = jnp.tril(jnp.ones((N_EXPERTS, N_EXPERTS), BF16), k=-1)
    causal = jnp.tril(jnp.ones((A_CHUNK, A_CHUNK), dtype=bool))
    b_up = moe_b_up.reshape(DEPTH, N_EXPERTS, 1, 2 * D_FF)
    b_down = moe_b_down.reshape(DEPTH, N_EXPERTS, 1, D_MODEL)

    for layer in range(DEPTH):
        i = layer // 2
        lg = ln_mix_g[layer].reshape(1, d)
        lb = ln_mix_b[layer].reshape(1, d)
        if layer % 2 == 0:
            w_s = jnp.where(causal, a_w_s[i], 0.0).astype(BF16)
            wcat = w_s.reshape(A_HEADS // 2, 2, A_CHUNK, A_CHUNK).transpose(0, 2, 1, 3)
            wcat = wcat.reshape(A_HEADS // 2, A_CHUNK, 2 * A_CHUNK)
            bias = jnp.repeat(a_b_s[i].T, A_HEAD_DIM, axis=1)
            h = _even_mixer(h, ab_w_in[i].astype(BF16), a_norm_g[i].reshape(1, A_WIDTH),
                            wcat, bias, b_conv_w[i], ab_w_out[i].astype(BF16), lg, lb)
        else:
            h = _odd_mixer(h, c_w_in[i].astype(BF16), c_w_grp[i].astype(BF16),
                           c_scale[i].reshape(1, d), c_w_out[i].astype(BF16), lg, lb)
        h = _moe(layer, h, _split_router_weight(router_w[layer]),
                 router_b[layer].reshape(N_EXPERTS, 1),
                 upper, lower, moe_w_up, b_up, moe_w_down, b_down,
                 ln_ffn_g[layer].reshape(1, d), ln_ffn_b[layer].reshape(1, d))
    return h.reshape(bsz, seq, d)
```

```python
import math

import jax
import jax.numpy as jnp
from jax import lax
from jax.experimental import pallas as pl
from jax.experimental.pallas import tpu as pltpu

F32 = jnp.float32
BF16 = jnp.bfloat16
I32 = jnp.int32

D_MODEL = 1024
DEPTH = 4
A_WIDTH = 512
A_HEADS = 8
A_HEAD_DIM = 64
A_CHUNK = 128
B_WIDTH = 512
PROJ_AB = 2 * A_WIDTH + 3 * B_WIDTH
C_GROUPS = 4
C_WINDOWS = (2, 4, 8, 16)
C_GROUP_DIM = 256
N_EXPERTS = 32
TOP_K = 4
D_FF = 1024
SWIGLU_LIMIT = 7.0
SWIGLU_ALPHA = 1.702
LN_EPS = 1e-5
DEEPNORM_ALPHA = float((2 * DEPTH) ** 0.25)

SUBLANES = 8
LANES = 128
MIX_TILE = 512
CONV_HALO = 8
POOL_HALO = 16
ROUTE_TILE = 256
EXPERT_BLOCK = 256
LOCAL_ROWS = ROUTE_TILE * TOP_K + N_EXPERTS * SUBLANES
LOCAL_GRANULES = LOCAL_ROWS // SUBLANES
VMEM_LIMIT = 56 * 1024 * 1024
EXPERT_DMA_CHUNKS = 8


def _layer_norm(x, g, b=None):
    mu = jnp.mean(x, axis=-1, keepdims=True)
    xc = x - mu
    var = jnp.mean(xc * xc, axis=-1, keepdims=True)
    y = xc * lax.rsqrt(var + LN_EPS) * g
    if b is not None:
        y = y + b
    return y


def _gelu_exact(x):
    return 0.5 * x * (1.0 + lax.erf(x * (1.0 / math.sqrt(2.0))))


def _route_tile(x, rw_ref, rb_ref, upper_ref, lower_ref):
    td = x.shape[0]
    x_hi = x.astype(BF16)
    x_lo = (x - x_hi.astype(F32)).astype(BF16)
    by_hi = jnp.dot(x_hi, rw_ref[...], preferred_element_type=F32)
    by_lo = jnp.dot(x_lo, rw_ref[:, :LANES], preferred_element_type=F32)
    token_major = by_hi[:, :LANES] + (by_hi[:, LANES:] + by_lo)
    logits = token_major.T[:N_EXPERTS, :] + rb_ref[...]
    eio = lax.broadcasted_iota(I32, (N_EXPERTS, td), 0)
    work = logits
    picks, vals = [], []
    for _ in range(TOP_K):
        m = jnp.max(work, axis=0, keepdims=True)
        idx = jnp.min(jnp.where(work == m, eio, N_EXPERTS), axis=0, keepdims=True)
        pick = eio == idx
        picks.append(pick)
        vals.append(m)
        work = jnp.where(pick, -jnp.inf, work)
    ex = [jnp.exp(v - vals[0]) for v in vals]
    den = ex[0] + ex[1] + ex[2] + ex[3]
    gates = [e / den for e in ex]

    onehot = jnp.zeros((N_EXPERTS, td), F32)
    for pick in picks:
        onehot = onehot + pick.astype(F32)
    rank = jnp.dot(onehot.astype(BF16), upper_ref[...], preferred_element_type=F32)
    cnt = jnp.sum(onehot, axis=1, keepdims=True).astype(I32)
    c8 = ((cnt + (SUBLANES - 1)) // SUBLANES) * SUBLANES
    c8b = jnp.broadcast_to(c8.astype(F32), (N_EXPERTS, LANES))
    loff = jnp.dot(lower_ref[...], c8b.astype(BF16), preferred_element_type=F32)[:, 0:1]
    slot = rank + loff
    rows = [jnp.sum(jnp.where(pick, slot, 0.0), axis=0, keepdims=True) for pick in picks]
    pad = jnp.zeros((SUBLANES - TOP_K, td), F32)
    return (jnp.concatenate(rows + [pad], axis=0).astype(I32),
            jnp.concatenate(gates + [pad], axis=0),
            jnp.broadcast_to(c8, (N_EXPERTS, LANES)))


def _finish_mixer_tile(x, mix, lg_ref, lb_ref, route_refs, o_ref, ls_ref, g_ref, c8_ref):
    x1 = _layer_norm(DEEPNORM_ALPHA * x + mix, lg_ref[...], lb_ref[...])
    o_ref[...] = x1
    for h in range(x1.shape[0] // ROUTE_TILE):
        cols = slice(h * ROUTE_TILE, (h + 1) * ROUTE_TILE)
        slots, gates, counts = _route_tile(x1[cols, :], *route_refs)
        ls_ref[:, cols] = slots
        g_ref[:, cols] = gates
        c8_ref[h] = counts


def _split_router_weight(router_w):
    hi = router_w.astype(BF16)
    lo = (router_w - hi.astype(F32)).astype(BF16)
    pad = ((0, 0), (0, LANES - N_EXPERTS))
    return jnp.concatenate([jnp.pad(hi, pad), jnp.pad(lo, pad)], axis=1)


def _even_mixer_kernel(x_ref, win_ref, ng_ref, wcat_ref, bias_ref, cw_ref, wout_ref,
                       lg_ref, lb_ref, rw_ref, rb_ref, upper_ref, lower_ref,
                       o_ref, ls_ref, g_ref, c8_ref, zc_ref):
    i = pl.program_id(0)
    tm = x_ref.shape[0]
    x = x_ref[...]
    z = jnp.dot(x.astype(BF16), win_ref[...], preferred_element_type=F32)
    u = _gelu_exact(z[:, :A_WIDTH])
    v = _gelu_exact(z[:, A_WIDTH:2 * A_WIDTH])
    g_b = z[:, 2 * A_WIDTH:2 * A_WIDTH + B_WIDTH]
    g_c = z[:, 2 * A_WIDTH + B_WIDTH:2 * A_WIDTH + 2 * B_WIDTH]
    x_in = z[:, 2 * A_WIDTH + 2 * B_WIDTH:]

    vb = _layer_norm(v, ng_ref[...]).astype(BF16)
    first_head = lax.broadcasted_iota(I32, (A_CHUNK, LANES), 1) < A_HEAD_DIM
    zero = jnp.zeros((), BF16)
    chunks = []
    for c in range(tm // A_CHUNK):
        slabs = []
        for j in range(A_WIDTH // LANES):
            slab = vb[c * A_CHUNK:(c + 1) * A_CHUNK, j * LANES:(j + 1) * LANES]
            rhs = jnp.concatenate([jnp.where(first_head, slab, zero),
                                   jnp.where(first_head, zero, slab)], axis=0)
            slabs.append(jnp.dot(wcat_ref[j], rhs, preferred_element_type=F32))
        chunks.append(jnp.concatenate(slabs, axis=1) + bias_ref[...])
    y_a = u * jnp.concatenate(chunks, axis=0)

    zz = g_c * x_in

    @pl.when(i == 0)
    def _():
        zc_ref[0:CONV_HALO, :] = jnp.zeros((CONV_HALO, B_WIDTH), F32)

    zc_ref[CONV_HALO:CONV_HALO + tm, :] = zz
    z1 = zc_ref[CONV_HALO - 1:CONV_HALO - 1 + tm, :]
    z2 = zc_ref[CONV_HALO - 2:CONV_HALO - 2 + tm, :]
    cw = cw_ref[...]
    conv = cw[0:1, :] * z2 + cw[1:2, :] * z1 + cw[2:3, :] * zz
    y_b = g_b * conv
    zc_ref[0:CONV_HALO, :] = zc_ref[tm:tm + CONV_HALO, :]

    y = jnp.concatenate([y_a, y_b], axis=1).astype(BF16)
    mix = jnp.dot(y, wout_ref[...], preferred_element_type=F32)
    _finish_mixer_tile(x, mix, lg_ref, lb_ref, (rw_ref, rb_ref, upper_ref, lower_ref),
                       o_ref, ls_ref, g_ref, c8_ref)


def _odd_mixer_kernel(x_ref, win_ref, wgrp_ref, scale_ref, wout_ref, lg_ref, lb_ref,
                      rw_ref, rb_ref, upper_ref, lower_ref,
                      o_ref, ls_ref, g_ref, c8_ref, zh_ref):
    i = pl.program_id(0)
    tm = x_ref.shape[0]
    x = x_ref[...]
    z = jnp.dot(x.astype(BF16), win_ref[...], preferred_element_type=F32)

    @pl.when(i == 0)
    def _():
        zh_ref[0:POOL_HALO, :] = jnp.zeros((POOL_HALO, D_MODEL), F32)

    zh_ref[POOL_HALO:POOL_HALO + tm, :] = z
    pos = i * tm + lax.broadcasted_iota(I32, (tm, 1), 0)
    mixed = []
    for g, win in enumerate(C_WINDOWS):
        cols = slice(g * C_GROUP_DIM, (g + 1) * C_GROUP_DIM)
        acc = zh_ref[:, cols]
        shift = 1
        while shift < win:
            acc = acc + pltpu.roll(acc, shift, axis=0)
            shift *= 2
        cnt = jnp.minimum(pos + 1, win).astype(F32)
        pooled = acc[POOL_HALO:, :] / cnt - z[:, cols]
        mixed.append(jnp.dot(pooled.astype(BF16), wgrp_ref[g], preferred_element_type=F32))
    zh_ref[0:POOL_HALO, :] = zh_ref[tm:tm + POOL_HALO, :]
    y = (jnp.concatenate(mixed, axis=1) * scale_ref[...]).astype(BF16)
    mix = jnp.dot(y, wout_ref[...], preferred_element_type=F32)
    _finish_mixer_tile(x, mix, lg_ref, lb_ref, (rw_ref, rb_ref, upper_ref, lower_ref),
                       o_ref, ls_ref, g_ref, c8_ref)


def _full(shape):
    return pl.BlockSpec(shape, lambda i: (0,) * len(shape))


def _mixer_call(body, name, scratch, x, params, route_params):
    t = x.shape[0]
    halves = MIX_TILE // ROUTE_TILE
    operands = tuple(params) + tuple(route_params)
    return pl.pallas_call(
        body,
        grid=(t // MIX_TILE,),
        in_specs=[pl.BlockSpec((MIX_TILE, D_MODEL), lambda i: (i, 0))]
                 + [_full(p.shape) for p in operands],
        out_specs=[pl.BlockSpec((MIX_TILE, D_MODEL), lambda i: (i, 0)),
                   pl.BlockSpec((SUBLANES, MIX_TILE), lambda i: (0, i)),
                   pl.BlockSpec((SUBLANES, MIX_TILE), lambda i: (0, i)),
                   pl.BlockSpec((halves, N_EXPERTS, LANES), lambda i: (i, 0, 0))],
        out_shape=[jax.ShapeDtypeStruct((t, D_MODEL), F32),
                   jax.ShapeDtypeStruct((SUBLANES, t), I32),
                   jax.ShapeDtypeStruct((SUBLANES, t), F32),
                   jax.ShapeDtypeStruct((t // ROUTE_TILE, N_EXPERTS, LANES), I32)],
        scratch_shapes=[scratch],
        compiler_params=pltpu.CompilerParams(dimension_semantics=("arbitrary",),
                                             vmem_limit_bytes=VMEM_LIMIT),
        name=name,
    )(x, *operands)


def _slot_matches(ls_ref, k):
    s_iota = lax.broadcasted_iota(I32, (LOCAL_ROWS, ROUTE_TILE), 0)
    return s_iota == ls_ref[k:k + 1, :]


def _for_each_granule(tab_s, ngr_s, tile, fn):
    def body(g, c):
        fn(pl.multiple_of(g * SUBLANES, SUBLANES),
           pl.multiple_of(tab_s[tile * LOCAL_GRANULES + g], SUBLANES))
        return c
    lax.fori_loop(0, ngr_s[tile], body, 0)


def _wait_granules(n_granules, copy_of_rows):
    for bit in range(LOCAL_GRANULES.bit_length()):
        @pl.when((lax.shift_right_logical(n_granules, bit) & 1) == 1)
        def _():
            copy_of_rows(SUBLANES << bit).wait()


def _dispatch_kernel(tab_s, ngr_s, tst_s, tn_s,
                     x_ref, ls_ref, xpad_ref, xs_ref, zero_ref, sem, zsem):
    i = pl.program_id(0)
    n = pl.num_programs(0)
    slot = i % 2

    def rows_copy(buf, src_row, dst_row, rows=SUBLANES):
        return pltpu.make_async_copy(xs_ref.at[buf, pl.ds(src_row, rows)],
                                     xpad_ref.at[pl.ds(dst_row, rows)], sem.at[buf])

    def wait_tile(tile, buf):
        _wait_granules(ngr_s[tile], lambda rows: rows_copy(buf, 0, 0, rows))

    def zero_copy(dst_row):
        return pltpu.make_async_copy(zero_ref, xpad_ref.at[pl.ds(dst_row, SUBLANES)], zsem)

    @pl.when(i == 0)
    def _():
        zero_ref[...] = jnp.zeros(zero_ref.shape, F32)

        def per_expert(e, total):
            def per_granule(g, c):
                zero_copy(pl.multiple_of(tst_s[e] + g * SUBLANES, SUBLANES)).start()
                return c
            lax.fori_loop(0, tn_s[e], per_granule, 0)
            return total + tn_s[e]

        total = lax.fori_loop(0, N_EXPERTS, per_expert, 0)

        def drain(_, c):
            zero_copy(0).wait()
            return c
        lax.fori_loop(0, total, drain, 0)

    match = _slot_matches(ls_ref, 0)
    for k in range(1, TOP_K):
        match = match | _slot_matches(ls_ref, k)
    perm = jnp.where(match, 1.0, 0.0).astype(BF16)
    xs_ref[slot] = jnp.dot(perm, x_ref[...].astype(BF16), preferred_element_type=F32)

    _for_each_granule(tab_s, ngr_s, i, lambda lo, go: rows_copy(slot, lo, go).start())

    @pl.when(i > 0)
    def _():
        wait_tile(i - 1, 1 - slot)

    @pl.when(i == n - 1)
    def _():
        wait_tile(i, slot)


def _dispatch(tables, x1, lslot, n_rows):
    t = x1.shape[0]
    n_tiles = t // ROUTE_TILE
    grid_spec = pltpu.PrefetchScalarGridSpec(
        num_scalar_prefetch=len(tables),
        grid=(n_tiles,),
        in_specs=[pl.BlockSpec((ROUTE_TILE, D_MODEL), lambda i, *_: (i, 0)),
                  pl.BlockSpec((SUBLANES, ROUTE_TILE), lambda i, *_: (0, i))],
        out_specs=pl.BlockSpec(memory_space=pl.ANY),
        scratch_shapes=[pltpu.VMEM((2, LOCAL_ROWS, D_MODEL), F32),
                        pltpu.VMEM((SUBLANES, D_MODEL), F32),
                        pltpu.SemaphoreType.DMA((2,)),
                        pltpu.SemaphoreType.DMA],
    )
    return pl.pallas_call(
        _dispatch_kernel,
        grid_spec=grid_spec,
        out_shape=jax.ShapeDtypeStruct((n_rows, D_MODEL), F32),
        compiler_params=pltpu.CompilerParams(dimension_semantics=("arbitrary",),
                                             vmem_limit_bytes=VMEM_LIMIT),
        name="dispatch",
    )(*tables, x1, lslot)


class _ChunkedCopy:
    def __init__(self, src, dst, sem):
        self.src, self.dst, self.sem = src, dst, sem

    def start(self):
        rows = EXPERT_BLOCK // EXPERT_DMA_CHUNKS
        for c in range(EXPERT_DMA_CHUNKS):
            part = pl.ds(c * rows, rows)
            pltpu.make_async_copy(self.src.at[part], self.dst.at[part],
                                  self.sem).start(priority=c % 2)

    def wait(self):
        pltpu.make_async_copy(self.src, self.dst, self.sem).wait()


def _expert_kernel(pstart_s, nblk_s, xpad_ref, wup_ref, bup_ref, wdn_ref, bdn_ref, ypad_ref,
                   wup_bf, wdn_bf, xbuf, ybuf, in_sem, out_sem):
    e = pl.program_id(0)
    nb = nblk_s[e]
    base = pstart_s[e]

    def block_rows(b):
        return pl.ds(pl.multiple_of(base + b * EXPERT_BLOCK, EXPERT_BLOCK), EXPERT_BLOCK)

    def in_copy(b, buf):
        return _ChunkedCopy(xpad_ref.at[block_rows(b)], xbuf.at[buf], in_sem.at[buf])

    def out_copy(b, buf):
        return _ChunkedCopy(ybuf.at[buf], ypad_ref.at[block_rows(b)], out_sem.at[buf])

    @pl.when(nb > 0)
    def _():
        in_copy(0, 0).start()
        wup_bf[...] = wup_ref[...].astype(BF16)
        wdn_bf[...] = wdn_ref[...].astype(BF16)

        def body(b, c):
            buf = b % 2

            @pl.when(b + 1 < nb)
            def _():
                in_copy(b + 1, 1 - buf).start()

            in_copy(b, buf).wait()

            @pl.when(b >= 2)
            def _():
                out_copy(b - 2, buf).wait()

            h = jnp.dot(xbuf[buf].astype(BF16), wup_bf[...],
                        preferred_element_type=F32) + bup_ref[...]
            gate = jnp.minimum(h[:, :D_FF], SWIGLU_LIMIT)
            lin = jnp.clip(h[:, D_FF:], -SWIGLU_LIMIT, SWIGLU_LIMIT)
            act = (lin + 1.0) * gate * jax.nn.sigmoid(SWIGLU_ALPHA * gate)
            ybuf[buf] = jnp.dot(act.astype(BF16), wdn_bf[...],
                                preferred_element_type=F32) + bdn_ref[...]
            out_copy(b, buf).start()
            return c

        lax.fori_loop(0, nb, body, 0)

        @pl.when(nb >= 2)
        def _():
            out_copy(nb - 2, nb % 2).wait()

        out_copy(nb - 1, (nb - 1) % 2).wait()


def _experts(layer, pstart, nblk, x_pad, w_up, b_up, w_down, b_down):
    def w_map(e, *_):
        return (layer, e, 0, 0)

    grid_spec = pltpu.PrefetchScalarGridSpec(
        num_scalar_prefetch=2,
        grid=(N_EXPERTS,),
        in_specs=[pl.BlockSpec(memory_space=pl.ANY),
                  pl.BlockSpec((None, None, D_MODEL, 2 * D_FF), w_map),
                  pl.BlockSpec((None, None, 1, 2 * D_FF), w_map),
                  pl.BlockSpec((None, None, D_FF, D_MODEL), w_map),
                  pl.BlockSpec((None, None, 1, D_MODEL), w_map)],
        out_specs=pl.BlockSpec(memory_space=pl.ANY),
        scratch_shapes=[pltpu.VMEM((D_MODEL, 2 * D_FF), BF16),
                        pltpu.VMEM((D_FF, D_MODEL), BF16),
                        pltpu.VMEM((2, EXPERT_BLOCK, D_MODEL), F32),
                        pltpu.VMEM((2, EXPERT_BLOCK, D_MODEL), F32),
                        pltpu.SemaphoreType.DMA((2,)),
                        pltpu.SemaphoreType.DMA((2,))],
    )
    return pl.pallas_call(
        _expert_kernel,
        grid_spec=grid_spec,
        out_shape=jax.ShapeDtypeStruct(x_pad.shape, F32),
        compiler_params=pltpu.CompilerParams(dimension_semantics=("arbitrary",),
                                             vmem_limit_bytes=VMEM_LIMIT),
        name="experts",
    )(pstart, nblk, x_pad, w_up, b_up, w_down, b_down)


def _combine_kernel(tab_s, ngr_s,
                    x_ref, ls_ref, g_ref, lg_ref, lb_ref, ypad_ref, o_ref, yl_ref, sem):
    i = pl.program_id(0)
    n = pl.num_programs(0)
    slot = i % 2

    def rows_copy(buf, src_row, dst_row, rows=SUBLANES):
        return pltpu.make_async_copy(ypad_ref.at[pl.ds(src_row, rows)],
                                     yl_ref.at[buf, pl.ds(dst_row, rows)], sem.at[buf])

    def fetch(tile, buf):
        _for_each_granule(tab_s, ngr_s, tile, lambda lo, go: rows_copy(buf, go, lo).start())

    @pl.when(i == 0)
    def _():
        yl_ref[...] = jnp.zeros(yl_ref.shape, F32)
        fetch(0, 0)

    @pl.when(i + 1 < n)
    def _():
        fetch(i + 1, 1 - slot)

    _wait_granules(ngr_s[i], lambda rows: rows_copy(slot, 0, 0, rows))

    gated = jnp.zeros((LOCAL_ROWS, ROUTE_TILE), F32)
    for k in range(TOP_K):
        gated = jnp.where(_slot_matches(ls_ref, k), g_ref[k:k + 1, :], gated)
    ffn = jnp.dot(gated.T.astype(BF16), yl_ref[slot].astype(BF16),
                  preferred_element_type=F32)
    o_ref[...] = _layer_norm(DEEPNORM_ALPHA * x_ref[...] + ffn, lg_ref[...], lb_ref[...])


def _combine(tables, x1, lslot, gates, lg, lb, y_pad):
    t = x1.shape[0]
    n_tiles = t // ROUTE_TILE
    grid_spec = pltpu.PrefetchScalarGridSpec(
        num_scalar_prefetch=len(tables),
        grid=(n_tiles,),
        in_specs=[pl.BlockSpec((ROUTE_TILE, D_MODEL), lambda i, *_: (i, 0)),
                  pl.BlockSpec((SUBLANES, ROUTE_TILE), lambda i, *_: (0, i)),
                  pl.BlockSpec((SUBLANES, ROUTE_TILE), lambda i, *_: (0, i)),
                  pl.BlockSpec(lg.shape, lambda i, *_: (0, 0)),
                  pl.BlockSpec(lb.shape, lambda i, *_: (0, 0)),
                  pl.BlockSpec(memory_space=pl.ANY)],
        out_specs=pl.BlockSpec((ROUTE_TILE, D_MODEL), lambda i, *_: (i, 0)),
        scratch_shapes=[pltpu.VMEM((2, LOCAL_ROWS, D_MODEL), F32),
                        pltpu.SemaphoreType.DMA((2,))],
    )
    return pl.pallas_call(
        _combine_kernel,
        grid_spec=grid_spec,
        out_shape=jax.ShapeDtypeStruct((t, D_MODEL), F32),
        compiler_params=pltpu.CompilerParams(dimension_semantics=("arbitrary",),
                                             vmem_limit_bytes=VMEM_LIMIT),
        name="combine",
    )(*tables, x1, lslot, gates, lg, lb, y_pad)


def _grouped_rows_bound(n_tokens):
    n_tiles = n_tokens // ROUTE_TILE
    worst = (n_tokens * TOP_K + n_tiles * N_EXPERTS * (SUBLANES - 1)
             + N_EXPERTS * (EXPERT_BLOCK - 1))
    return -(-worst // EXPERT_BLOCK) * EXPERT_BLOCK


def _routing_tables(c8):
    seg_total = jnp.sum(c8, axis=0)
    padded = (seg_total + EXPERT_BLOCK - 1) // EXPERT_BLOCK * EXPERT_BLOCK
    pstart = jnp.cumsum(padded) - padded
    gstart = pstart[None, :] + jnp.cumsum(c8, axis=0) - c8
    lend = jnp.cumsum(c8, axis=1)
    g_row = jnp.arange(LOCAL_GRANULES, dtype=I32) * SUBLANES
    loff = lend - c8
    row = g_row[None, :, None]
    inside = (loff[:, None, :] <= row) & (row < lend[:, None, :])
    shift = jnp.sum(jnp.where(inside, (gstart - loff)[:, None, :], 0), axis=2)
    granule_dst = shift + g_row[None, :]
    ngran = lend[:, -1] // SUBLANES
    tail_start = pstart + seg_total
    tail_n = (padded - seg_total) // SUBLANES
    as_i32 = lambda a: a.astype(I32)
    return ((as_i32(granule_dst.reshape(-1)), as_i32(ngran)),
            (as_i32(tail_start), as_i32(tail_n)),
            (as_i32(pstart), as_i32(padded // EXPERT_BLOCK)))


def _moe(layer, x1, lslot, gates, counts, w_up, b_up, w_down, b_down, lg, lb):
    t = x1.shape[0]
    tile_tables, tail_tables, expert_tables = _routing_tables(counts[:, :, 0])
    x_pad = _dispatch(tile_tables + tail_tables, x1, lslot, _grouped_rows_bound(t))
    y_pad = _experts(layer, *expert_tables, x_pad, w_up, b_up, w_down, b_down)
    return _combine(tile_tables, x1, lslot, gates, lg, lb, y_pad)


def kernel(x, ab_w_in, a_norm_g, a_w_s, a_b_s, b_conv_w, ab_w_out, c_w_in, c_w_grp, c_scale,
           c_w_out, ln_mix_g, ln_mix_b, router_w, router_b, moe_w_up, moe_b_up, moe_w_down,
           moe_b_down, ln_ffn_g, ln_ffn_b):
    bsz, seq, d = x.shape
    assert d == D_MODEL and (bsz * seq) % MIX_TILE == 0 and seq % A_CHUNK == 0
    assert bsz == 1, "token tiles carry conv / pooling state across the sequence"
    h = x.reshape(bsz * seq, d)

    upper = jnp.triu(jnp.ones((ROUTE_TILE, ROUTE_TILE), BF16), k=1)
    lower = jnp.tril(jnp.ones((N_EXPERTS, N_EXPERTS), BF16), k=-1)
    causal = jnp.tril(jnp.ones((A_CHUNK, A_CHUNK), dtype=bool))
    b_up = moe_b_up.reshape(DEPTH, N_EXPERTS, 1, 2 * D_FF)
    b_down = moe_b_down.reshape(DEPTH, N_EXPERTS, 1, D_MODEL)

    for layer in range(DEPTH):
        i = layer // 2
        lg = ln_mix_g[layer].reshape(1, d)
        lb = ln_mix_b[layer].reshape(1, d)
        route_params = (_split_router_weight(router_w[layer]),
                        router_b[layer].reshape(N_EXPERTS, 1), upper, lower)
        if layer % 2 == 0:
            w_s = jnp.where(causal, a_w_s[i], 0.0).astype(BF16)
            wcat = w_s.reshape(A_HEADS // 2, 2, A_CHUNK, A_CHUNK).transpose(0, 2, 1, 3)
            wcat = wcat.reshape(A_HEADS // 2, A_CHUNK, 2 * A_CHUNK)
            bias = jnp.repeat(a_b_s[i].T, A_HEAD_DIM, axis=1)
            params = (ab_w_in[i].astype(BF16), a_norm_g[i].reshape(1, A_WIDTH), wcat, bias,
                      b_conv_w[i], ab_w_out[i].astype(BF16), lg, lb)
            mixed = _mixer_call(_even_mixer_kernel, "even_mixer",
                                pltpu.VMEM((MIX_TILE + CONV_HALO, B_WIDTH), F32),
                                h, params, route_params)
        else:
            params = (c_w_in[i].astype(BF16), c_w_grp[i].astype(BF16),
                      c_scale[i].reshape(1, d), c_w_out[i].astype(BF16), lg, lb)
            mixed = _mixer_call(_odd_mixer_kernel, "odd_mixer",
                                pltpu.VMEM((MIX_TILE + POOL_HALO, D_MODEL), F32),
                                h, params, route_params)
        h = _moe(layer, *mixed, moe_w_up, b_up, moe_w_down, b_down,
                 ln_ffn_g[layer].reshape(1, d), ln_ffn_b[layer].reshape(1, d))
    return h.reshape(bsz, seq, d)
```

```python
import math

import jax
import jax.numpy as jnp
from jax import lax
from jax.experimental import pallas as pl
from jax.experimental.pallas import tpu as pltpu

F32 = jnp.float32
BF16 = jnp.bfloat16
I32 = jnp.int32

D_MODEL = 1024
DEPTH = 4
A_WIDTH = 512
A_HEADS = 8
A_HEAD_DIM = 64
A_CHUNK = 128
B_WIDTH = 512
PROJ_AB = 2 * A_WIDTH + 3 * B_WIDTH
C_GROUPS = 4
C_WINDOWS = (2, 4, 8, 16)
C_GROUP_DIM = 256
N_EXPERTS = 32
TOP_K = 4
D_FF = 1024
SWIGLU_LIMIT = 7.0
SWIGLU_ALPHA = 1.702
LN_EPS = 1e-5
DEEPNORM_ALPHA = float((2 * DEPTH) ** 0.25)

SUBLANES = 8
LANES = 128
MIX_TILE = 512
CONV_HALO = 8
POOL_HALO = 16
ROUTE_TILE = 256
EXPERT_BLOCK = 256
LOCAL_ROWS = ROUTE_TILE * TOP_K + N_EXPERTS * SUBLANES
LOCAL_GRANULES = LOCAL_ROWS // SUBLANES
VMEM_LIMIT = 56 * 1024 * 1024
EXPERT_GROUP = 2


def _layer_norm(x, g, b=None):
    mu = jnp.mean(x, axis=-1, keepdims=True)
    xc = x - mu
    var = jnp.mean(xc * xc, axis=-1, keepdims=True)
    y = xc * lax.rsqrt(var + LN_EPS) * g
    if b is not None:
        y = y + b
    return y


def _gelu_exact(x):
    return 0.5 * x * (1.0 + lax.erf(x * (1.0 / math.sqrt(2.0))))


def _route_tile(x, rw_ref, rb_ref, upper_ref, lower_ref):
    td = x.shape[0]
    x_hi = x.astype(BF16)
    x_lo = (x - x_hi.astype(F32)).astype(BF16)
    by_hi = jnp.dot(x_hi, rw_ref[...], preferred_element_type=F32)
    by_lo = jnp.dot(x_lo, rw_ref[:, :LANES], preferred_element_type=F32)
    token_major = by_hi[:, :LANES] + (by_hi[:, LANES:] + by_lo)
    logits = token_major.T[:N_EXPERTS, :] + rb_ref[...]
    eio = lax.broadcasted_iota(I32, (N_EXPERTS, td), 0)
    work = logits
    picks, vals = [], []
    for _ in range(TOP_K):
        m = jnp.max(work, axis=0, keepdims=True)
        idx = jnp.min(jnp.where(work == m, eio, N_EXPERTS), axis=0, keepdims=True)
        pick = eio == idx
        picks.append(pick)
        vals.append(m)
        work = jnp.where(pick, -jnp.inf, work)
    ex = [jnp.exp(v - vals[0]) for v in vals]
    den = ex[0] + ex[1] + ex[2] + ex[3]
    gates = [e / den for e in ex]

    onehot = jnp.zeros((N_EXPERTS, td), F32)
    for pick in picks:
        onehot = onehot + pick.astype(F32)
    rank = jnp.dot(onehot.astype(BF16), upper_ref[...], preferred_element_type=F32)
    cnt = jnp.sum(onehot, axis=1, keepdims=True).astype(I32)
    c8 = ((cnt + (SUBLANES - 1)) // SUBLANES) * SUBLANES
    c8b = jnp.broadcast_to(c8.astype(F32), (N_EXPERTS, LANES))
    loff = jnp.dot(lower_ref[...], c8b.astype(BF16), preferred_element_type=F32)[:, 0:1]
    slot = rank + loff
    rows = [jnp.sum(jnp.where(pick, slot, 0.0), axis=0, keepdims=True) for pick in picks]
    pad = jnp.zeros((SUBLANES - TOP_K, td), F32)
    return (jnp.concatenate(rows + [pad], axis=0).astype(I32),
            jnp.concatenate(gates + [pad], axis=0),
            jnp.broadcast_to(c8, (N_EXPERTS, LANES)))


def _finish_mixer_tile(x, mix, lg_ref, lb_ref, route_refs, o_ref, ls_ref, g_ref, c8_ref):
    x1 = _layer_norm(DEEPNORM_ALPHA * x + mix, lg_ref[...], lb_ref[...])
    o_ref[...] = x1
    for h in range(x1.shape[0] // ROUTE_TILE):
        cols = slice(h * ROUTE_TILE, (h + 1) * ROUTE_TILE)
        slots, gates, counts = _route_tile(x1[cols, :], *route_refs)
        ls_ref[:, cols] = slots
        g_ref[:, cols] = gates
        c8_ref[h] = counts


def _split_router_weight(router_w):
    hi = router_w.astype(BF16)
    lo = (router_w - hi.astype(F32)).astype(BF16)
    pad = ((0, 0), (0, LANES - N_EXPERTS))
    return jnp.concatenate([jnp.pad(hi, pad), jnp.pad(lo, pad)], axis=1)


def _even_mixer_kernel(x_ref, win_ref, ng_ref, wcat_ref, bias_ref, cw_ref, wout_ref,
                       lg_ref, lb_ref, rw_ref, rb_ref, upper_ref, lower_ref,
                       o_ref, ls_ref, g_ref, c8_ref, zc_ref):
    i = pl.program_id(0)
    tm = x_ref.shape[0]
    x = x_ref[...]
    z = jnp.dot(x.astype(BF16), win_ref[...], preferred_element_type=F32)
    u = _gelu_exact(z[:, :A_WIDTH])
    v = _gelu_exact(z[:, A_WIDTH:2 * A_WIDTH])
    g_b = z[:, 2 * A_WIDTH:2 * A_WIDTH + B_WIDTH]
    g_c = z[:, 2 * A_WIDTH + B_WIDTH:2 * A_WIDTH + 2 * B_WIDTH]
    x_in = z[:, 2 * A_WIDTH + 2 * B_WIDTH:]

    vb = _layer_norm(v, ng_ref[...]).astype(BF16)
    first_head = lax.broadcasted_iota(I32, (A_CHUNK, LANES), 1) < A_HEAD_DIM
    zero = jnp.zeros((), BF16)
    chunks = []
    for c in range(tm // A_CHUNK):
        slabs = []
        for j in range(A_WIDTH // LANES):
            slab = vb[c * A_CHUNK:(c + 1) * A_CHUNK, j * LANES:(j + 1) * LANES]
            rhs = jnp.concatenate([jnp.where(first_head, slab, zero),
                                   jnp.where(first_head, zero, slab)], axis=0)
            slabs.append(jnp.dot(wcat_ref[j], rhs, preferred_element_type=F32))
        chunks.append(jnp.concatenate(slabs, axis=1) + bias_ref[...])
    y_a = u * jnp.concatenate(chunks, axis=0)

    zz = g_c * x_in

    @pl.when(i == 0)
    def _():
        zc_ref[0:CONV_HALO, :] = jnp.zeros((CONV_HALO, B_WIDTH), F32)

    zc_ref[CONV_HALO:CONV_HALO + tm, :] = zz
    z1 = zc_ref[CONV_HALO - 1:CONV_HALO - 1 + tm, :]
    z2 = zc_ref[CONV_HALO - 2:CONV_HALO - 2 + tm, :]
    cw = cw_ref[...]
    conv = cw[0:1, :] * z2 + cw[1:2, :] * z1 + cw[2:3, :] * zz
    y_b = g_b * conv
    zc_ref[0:CONV_HALO, :] = zc_ref[tm:tm + CONV_HALO, :]

    y = jnp.concatenate([y_a, y_b], axis=1).astype(BF16)
    mix = jnp.dot(y, wout_ref[...], preferred_element_type=F32)
    _finish_mixer_tile(x, mix, lg_ref, lb_ref, (rw_ref, rb_ref, upper_ref, lower_ref),
                       o_ref, ls_ref, g_ref, c8_ref)


def _odd_mixer_kernel(x_ref, win_ref, wgrp_ref, scale_ref, wout_ref, lg_ref, lb_ref,
                      rw_ref, rb_ref, upper_ref, lower_ref,
                      o_ref, ls_ref, g_ref, c8_ref, zh_ref):
    i = pl.program_id(0)
    tm = x_ref.shape[0]
    x = x_ref[...]
    z = jnp.dot(x.astype(BF16), win_ref[...], preferred_element_type=F32)

    @pl.when(i == 0)
    def _():
        zh_ref[0:POOL_HALO, :] = jnp.zeros((POOL_HALO, D_MODEL), F32)

    zh_ref[POOL_HALO:POOL_HALO + tm, :] = z
    pos = i * tm + lax.broadcasted_iota(I32, (tm, 1), 0)
    mixed = []
    for g, win in enumerate(C_WINDOWS):
        cols = slice(g * C_GROUP_DIM, (g + 1) * C_GROUP_DIM)
        acc = zh_ref[:, cols]
        shift = 1
        while shift < win:
            acc = acc + pltpu.roll(acc, shift, axis=0)
            shift *= 2
        cnt = jnp.minimum(pos + 1, win).astype(F32)
        pooled = acc[POOL_HALO:, :] / cnt - z[:, cols]
        mixed.append(jnp.dot(pooled.astype(BF16), wgrp_ref[g], preferred_element_type=F32))
    zh_ref[0:POOL_HALO, :] = zh_ref[tm:tm + POOL_HALO, :]
    y = (jnp.concatenate(mixed, axis=1) * scale_ref[...]).astype(BF16)
    mix = jnp.dot(y, wout_ref[...], preferred_element_type=F32)
    _finish_mixer_tile(x, mix, lg_ref, lb_ref, (rw_ref, rb_ref, upper_ref, lower_ref),
                       o_ref, ls_ref, g_ref, c8_ref)


def _full(shape):
    return pl.BlockSpec(shape, lambda i: (0,) * len(shape))


def _mixer_call(body, name, scratch, x, params, route_params):
    t = x.shape[0]
    halves = MIX_TILE // ROUTE_TILE
    operands = tuple(params) + tuple(route_params)
    return pl.pallas_call(
        body,
        grid=(t // MIX_TILE,),
        in_specs=[pl.BlockSpec((MIX_TILE, D_MODEL), lambda i: (i, 0))]
                 + [_full(p.shape) for p in operands],
        out_specs=[pl.BlockSpec((MIX_TILE, D_MODEL), lambda i: (i, 0)),
                   pl.BlockSpec((SUBLANES, MIX_TILE), lambda i: (0, i)),
                   pl.BlockSpec((SUBLANES, MIX_TILE), lambda i: (0, i)),
                   pl.BlockSpec((halves, N_EXPERTS, LANES), lambda i: (i, 0, 0))],
        out_shape=[jax.ShapeDtypeStruct((t, D_MODEL), F32),
                   jax.ShapeDtypeStruct((SUBLANES, t), I32),
                   jax.ShapeDtypeStruct((SUBLANES, t), F32),
                   jax.ShapeDtypeStruct((t // ROUTE_TILE, N_EXPERTS, LANES), I32)],
        scratch_shapes=[scratch],
        compiler_params=pltpu.CompilerParams(dimension_semantics=("arbitrary",),
                                             vmem_limit_bytes=VMEM_LIMIT),
        name=name,
    )(x, *operands)


def _slot_matches(ls_ref, k):
    s_iota = lax.broadcasted_iota(I32, (LOCAL_ROWS, ROUTE_TILE), 0)
    return s_iota == ls_ref[k:k + 1, :]


def _for_each_granule(tab_s, tile, fn):
    for g in range(LOCAL_GRANULES):
        fn(g * SUBLANES, pl.multiple_of(tab_s[tile * LOCAL_GRANULES + g], SUBLANES))


def _dispatch_kernel(tab_s, tst_s, tn_s,
                     x_ref, ls_ref, xpad_ref, xs_ref, zero_ref, sem, zsem):
    i = pl.program_id(0)
    n = pl.num_programs(0)
    slot = i % 2

    def rows_copy(buf, src_row, dst_row, rows=SUBLANES):
        return pltpu.make_async_copy(xs_ref.at[buf, pl.ds(src_row, rows)],
                                     xpad_ref.at[pl.ds(dst_row, rows)], sem.at[buf])

    def wait_tile(buf):
        rows_copy(buf, 0, 0, LOCAL_ROWS).wait()

    def zero_copy(dst_row):
        return pltpu.make_async_copy(zero_ref, xpad_ref.at[pl.ds(dst_row, SUBLANES)], zsem)

    @pl.when(i == 0)
    def _():
        zero_ref[...] = jnp.zeros(zero_ref.shape, F32)

        def per_expert(e, total):
            def per_granule(g, c):
                zero_copy(pl.multiple_of(tst_s[e] + g * SUBLANES, SUBLANES)).start()
                return c
            lax.fori_loop(0, tn_s[e], per_granule, 0)
            return total + tn_s[e]

        total = lax.fori_loop(0, N_EXPERTS, per_expert, 0)

        def drain(_, c):
            zero_copy(0).wait()
            return c
        lax.fori_loop(0, total, drain, 0)

    match = _slot_matches(ls_ref, 0)
    for k in range(1, TOP_K):
        match = match | _slot_matches(ls_ref, k)
    perm = jnp.where(match, 1.0, 0.0).astype(BF16)
    xs_ref[slot] = jnp.dot(perm, x_ref[...].astype(BF16), preferred_element_type=F32)

    _for_each_granule(tab_s, i, lambda lo, go: rows_copy(slot, lo, go).start())

    @pl.when(i > 0)
    def _():
        wait_tile(1 - slot)

    @pl.when(i == n - 1)
    def _():
        wait_tile(slot)


def _dispatch(tables, x1, lslot, n_rows):
    t = x1.shape[0]
    n_tiles = t // ROUTE_TILE
    grid_spec = pltpu.PrefetchScalarGridSpec(
        num_scalar_prefetch=len(tables),
        grid=(n_tiles,),
        in_specs=[pl.BlockSpec((ROUTE_TILE, D_MODEL), lambda i, *_: (i, 0)),
                  pl.BlockSpec((SUBLANES, ROUTE_TILE), lambda i, *_: (0, i))],
        out_specs=pl.BlockSpec(memory_space=pl.ANY),
        scratch_shapes=[pltpu.VMEM((2, LOCAL_ROWS, D_MODEL), F32),
                        pltpu.VMEM((SUBLANES, D_MODEL), F32),
                        pltpu.SemaphoreType.DMA((2,)),
                        pltpu.SemaphoreType.DMA],
    )
    return pl.pallas_call(
        _dispatch_kernel,
        grid_spec=grid_spec,
        out_shape=jax.ShapeDtypeStruct((n_rows, D_MODEL), F32),
        compiler_params=pltpu.CompilerParams(dimension_semantics=("arbitrary",),
                                             vmem_limit_bytes=VMEM_LIMIT),
        name="dispatch",
    )(*tables, x1, lslot)


def _expert_kernel(pstart_s, nblk_s, xpad_ref, wup_ref, bup_ref, wdn_ref, bdn_ref, ypad_ref,
                   wup_bf, wdn_bf, xbuf, ybuf, in_sem, out_sem):
    e = pl.program_id(0)
    nb = nblk_s[e]
    ng = lax.shift_right_logical(nb + (EXPERT_GROUP - 1), EXPERT_GROUP.bit_length() - 1)
    base = pstart_s[e]
    group_rows = EXPERT_GROUP * EXPERT_BLOCK

    def rows_of(g, n_rows):
        return pl.ds(pl.multiple_of(base + g * group_rows, EXPERT_BLOCK), n_rows)

    def in_copy(g, buf, n_blocks):
        n_rows = n_blocks * EXPERT_BLOCK
        return pltpu.make_async_copy(xpad_ref.at[rows_of(g, n_rows)],
                                     xbuf.at[buf, pl.ds(0, n_rows)], in_sem.at[buf])

    def out_copy(g, buf, n_blocks):
        n_rows = n_blocks * EXPERT_BLOCK
        return pltpu.make_async_copy(ybuf.at[buf, pl.ds(0, n_rows)],
                                     ypad_ref.at[rows_of(g, n_rows)], out_sem.at[buf])

    def blocks_in(g):
        return jnp.minimum(nb - g * EXPERT_GROUP, EXPERT_GROUP)

    def for_group_size(g, fn):
        for n_blocks in range(1, EXPERT_GROUP + 1):
            @pl.when(blocks_in(g) == n_blocks)
            def _():
                fn(n_blocks)

    def start_fetch(g):
        for_group_size(g, lambda n_blocks: in_copy(g, g % 3, n_blocks).start())

    @pl.when(nb > 0)
    def _():
        start_fetch(0)

        @pl.when(ng > 1)
        def _():
            start_fetch(1)

        wup_bf[...] = wup_ref[...].astype(BF16)
        wdn_bf[...] = wdn_ref[...].astype(BF16)

        def body(g, c):
            ibuf = g % 3
            obuf = g % 2

            @pl.when(g + 2 < ng)
            def _():
                start_fetch(g + 2)

            for_group_size(g, lambda n_blocks: in_copy(g, ibuf, n_blocks).wait())

            @pl.when(g >= 2)
            def _():
                out_copy(g - 2, obuf, EXPERT_GROUP).wait()

            def compute_and_store(n_blocks):
                rows = pl.ds(0, n_blocks * EXPERT_BLOCK)
                h = jnp.dot(xbuf[ibuf, rows, :].astype(BF16), wup_bf[...],
                            preferred_element_type=F32) + bup_ref[...]
                gate = jnp.minimum(h[:, :D_FF], SWIGLU_LIMIT)
                lin = jnp.clip(h[:, D_FF:], -SWIGLU_LIMIT, SWIGLU_LIMIT)
                act = (lin + 1.0) * gate * jax.nn.sigmoid(SWIGLU_ALPHA * gate)
                ybuf[obuf, rows, :] = jnp.dot(act.astype(BF16), wdn_bf[...],
                                              preferred_element_type=F32) + bdn_ref[...]
                out_copy(g, obuf, n_blocks).start()

            for_group_size(g, compute_and_store)
            return c

        lax.fori_loop(0, ng, body, 0)

        @pl.when(ng >= 2)
        def _():
            out_copy(ng - 2, ng % 2, EXPERT_GROUP).wait()

        for_group_size(ng - 1, lambda n_blocks: out_copy(ng - 1, (ng - 1) % 2, n_blocks).wait())


def _experts(layer, pstart, nblk, x_pad, w_up, b_up, w_down, b_down):
    def w_map(e, *_):
        return (layer, e, 0, 0)

    grid_spec = pltpu.PrefetchScalarGridSpec(
        num_scalar_prefetch=2,
        grid=(N_EXPERTS,),
        in_specs=[pl.BlockSpec(memory_space=pl.ANY),
                  pl.BlockSpec((None, None, D_MODEL, 2 * D_FF), w_map),
                  pl.BlockSpec((None, None, 1, 2 * D_FF), w_map),
                  pl.BlockSpec((None, None, D_FF, D_MODEL), w_map),
                  pl.BlockSpec((None, None, 1, D_MODEL), w_map)],
        out_specs=pl.BlockSpec(memory_space=pl.ANY),
        scratch_shapes=[pltpu.VMEM((D_MODEL, 2 * D_FF), BF16),
                        pltpu.VMEM((D_FF, D_MODEL), BF16),
                        pltpu.VMEM((3, EXPERT_GROUP * EXPERT_BLOCK, D_MODEL), F32),
                        pltpu.VMEM((2, EXPERT_GROUP * EXPERT_BLOCK, D_MODEL), F32),
                        pltpu.SemaphoreType.DMA((3,)),
                        pltpu.SemaphoreType.DMA((2,))],
    )
    return pl.pallas_call(
        _expert_kernel,
        grid_spec=grid_spec,
        out_shape=jax.ShapeDtypeStruct(x_pad.shape, F32),
        compiler_params=pltpu.CompilerParams(dimension_semantics=("arbitrary",),
                                             vmem_limit_bytes=VMEM_LIMIT),
        name="experts",
    )(pstart, nblk, x_pad, w_up, b_up, w_down, b_down)


def _combine_kernel(tab_s,
                    x_ref, ls_ref, g_ref, lg_ref, lb_ref, ypad_ref, o_ref, yl_ref, sem):
    i = pl.program_id(0)
    n = pl.num_programs(0)
    slot = i % 2

    def rows_copy(buf, src_row, dst_row, rows=SUBLANES):
        return pltpu.make_async_copy(ypad_ref.at[pl.ds(src_row, rows)],
                                     yl_ref.at[buf, pl.ds(dst_row, rows)], sem.at[buf])

    def fetch(tile, buf):
        _for_each_granule(tab_s, tile, lambda lo, go: rows_copy(buf, go, lo).start())

    @pl.when(i == 0)
    def _():
        fetch(0, 0)

    @pl.when(i + 1 < n)
    def _():
        fetch(i + 1, 1 - slot)

    rows_copy(slot, 0, 0, LOCAL_ROWS).wait()

    gated = jnp.zeros((LOCAL_ROWS, ROUTE_TILE), F32)
    for k in range(TOP_K):
        gated = jnp.where(_slot_matches(ls_ref, k), g_ref[k:k + 1, :], gated)
    ffn = jnp.dot(gated.T.astype(BF16), yl_ref[slot].astype(BF16),
                  preferred_element_type=F32)
    o_ref[...] = _layer_norm(DEEPNORM_ALPHA * x_ref[...] + ffn, lg_ref[...], lb_ref[...])


def _combine(tables, x1, lslot, gates, lg, lb, y_pad):
    t = x1.shape[0]
    n_tiles = t // ROUTE_TILE
    grid_spec = pltpu.PrefetchScalarGridSpec(
        num_scalar_prefetch=len(tables),
        grid=(n_tiles,),
        in_specs=[pl.BlockSpec((ROUTE_TILE, D_MODEL), lambda i, *_: (i, 0)),
                  pl.BlockSpec((SUBLANES, ROUTE_TILE), lambda i, *_: (0, i)),
                  pl.BlockSpec((SUBLANES, ROUTE_TILE), lambda i, *_: (0, i)),
                  pl.BlockSpec(lg.shape, lambda i, *_: (0, 0)),
                  pl.BlockSpec(lb.shape, lambda i, *_: (0, 0)),
                  pl.BlockSpec(memory_space=pl.ANY)],
        out_specs=pl.BlockSpec((ROUTE_TILE, D_MODEL), lambda i, *_: (i, 0)),
        scratch_shapes=[pltpu.VMEM((2, LOCAL_ROWS, D_MODEL), F32),
                        pltpu.SemaphoreType.DMA((2,))],
    )
    return pl.pallas_call(
        _combine_kernel,
        grid_spec=grid_spec,
        out_shape=jax.ShapeDtypeStruct((t, D_MODEL), F32),
        compiler_params=pltpu.CompilerParams(dimension_semantics=("arbitrary",),
                                             vmem_limit_bytes=VMEM_LIMIT),
        name="combine",
    )(*tables, x1, lslot, gates, lg, lb, y_pad)


def _grouped_rows_bound(n_tokens):
    n_tiles = n_tokens // ROUTE_TILE
    worst = (n_tokens * TOP_K + n_tiles * N_EXPERTS * (SUBLANES - 1)
             + N_EXPERTS * (EXPERT_BLOCK - 1))
    return -(-worst // EXPERT_BLOCK) * EXPERT_BLOCK


def _routing_tables(c8, spare_base):
    n_tiles = c8.shape[0]
    seg_total = jnp.sum(c8, axis=0)
    padded = (seg_total + EXPERT_BLOCK - 1) // EXPERT_BLOCK * EXPERT_BLOCK
    pstart = jnp.cumsum(padded) - padded
    gstart = pstart[None, :] + jnp.cumsum(c8, axis=0) - c8
    lend = jnp.cumsum(c8, axis=1)
    g_row = jnp.arange(LOCAL_GRANULES, dtype=I32) * SUBLANES
    loff = lend - c8
    row = g_row[None, :, None]
    inside = (loff[:, None, :] <= row) & (row < lend[:, None, :])
    shift = jnp.sum(jnp.where(inside, (gstart - loff)[:, None, :], 0), axis=2)
    granule_row = shift + g_row[None, :]
    occupied = g_row[None, :] < lend[:, -1:]
    spare = (spare_base + (jnp.arange(n_tiles, dtype=I32) % 2)[:, None] * LOCAL_ROWS
             + g_row[None, :])
    dispatch_dst = jnp.where(occupied, granule_row, spare)
    combine_src = jnp.where(occupied, granule_row, 0)
    tail_start = pstart + seg_total
    tail_n = (padded - seg_total) // SUBLANES
    as_i32 = lambda a: a.astype(I32)
    return ((as_i32(dispatch_dst.reshape(-1)), as_i32(tail_start), as_i32(tail_n)),
            (as_i32(pstart), as_i32(padded // EXPERT_BLOCK)),
            (as_i32(combine_src.reshape(-1)),))


def _moe(layer, x1, lslot, gates, counts, w_up, b_up, w_down, b_down, lg, lb):
    spare_base = _grouped_rows_bound(x1.shape[0])
    dispatch_tables, expert_tables, combine_tables = _routing_tables(counts[:, :, 0],
                                                                     spare_base)
    x_pad = _dispatch(dispatch_tables, x1, lslot, spare_base + 2 * LOCAL_ROWS)
    y_pad = _experts(layer, *expert_tables, x_pad, w_up, b_up, w_down, b_down)
    return _combine(combine_tables, x1, lslot, gates, lg, lb, y_pad)


def kernel(x, ab_w_in, a_norm_g, a_w_s, a_b_s, b_conv_w, ab_w_out, c_w_in, c_w_grp, c_scale,
           c_w_out, ln_mix_g, ln_mix_b, router_w, router_b, moe_w_up, moe_b_up, moe_w_down,
           moe_b_down, ln_ffn_g, ln_ffn_b):
    bsz, seq, d = x.shape
    assert d == D_MODEL and (bsz * seq) % MIX_TILE == 0 and seq % A_CHUNK == 0
    assert bsz == 1, "token tiles carry conv / pooling state across the sequence"
    h = x.reshape(bsz * seq, d)

    upper = jnp.triu(jnp.ones((ROUTE_TILE, ROUTE_TILE), BF16), k=1)
    lower = jnp.tril(jnp.ones((N_EXPERTS, N_EXPERTS), BF16), k=-1)
    causal = jnp.tril(jnp.ones((A_CHUNK, A_CHUNK), dtype=bool))
    b_up = moe_b_up.reshape(DEPTH, N_EXPERTS, 1, 2 * D_FF)
    b_down = moe_b_down.reshape(DEPTH, N_EXPERTS, 1, D_MODEL)

    for layer in range(DEPTH):
        i = layer // 2
        lg = ln_mix_g[layer].reshape(1, d)
        lb = ln_mix_b[layer].reshape(1, d)
        route_params = (_split_router_weight(router_w[layer]),
                        router_b[layer].reshape(N_EXPERTS, 1), upper, lower)
        if layer % 2 == 0:
            w_s = jnp.where(causal, a_w_s[i], 0.0).astype(BF16)
            wcat = w_s.reshape(A_HEADS // 2, 2, A_CHUNK, A_CHUNK).transpose(0, 2, 1, 3)
            wcat = wcat.reshape(A_HEADS // 2, A_CHUNK, 2 * A_CHUNK)
            bias = jnp.repeat(a_b_s[i].T, A_HEAD_DIM, axis=1)
            params = (ab_w_in[i].astype(BF16), a_norm_g[i].reshape(1, A_WIDTH), wcat, bias,
                      b_conv_w[i], ab_w_out[i].astype(BF16), lg, lb)
            mixed = _mixer_call(_even_mixer_kernel, "even_mixer",
                                pltpu.VMEM((MIX_TILE + CONV_HALO, B_WIDTH), F32),
                                h, params, route_params)
        else:
            params = (c_w_in[i].astype(BF16), c_w_grp[i].astype(BF16),
                      c_scale[i].reshape(1, d), c_w_out[i].astype(BF16), lg, lb)
            mixed = _mixer_call(_odd_mixer_kernel, "odd_mixer",
                                pltpu.VMEM((MIX_TILE + POOL_HALO, D_MODEL), F32),
                                h, params, route_params)
        h = _moe(layer, *mixed, moe_w_up, b_up, moe_w_down, b_down,
                 ln_ffn_g[layer].reshape(1, d), ln_ffn_b[layer].reshape(1, d))
    return h.reshape(bsz, seq, d)
```

```python
import math

import jax
import jax.numpy as jnp
from jax import lax
from jax.experimental import pallas as pl
from jax.experimental.pallas import tpu as pltpu

F32 = jnp.float32
BF16 = jnp.bfloat16
I32 = jnp.int32

D_MODEL = 1024
DEPTH = 4
A_WIDTH = 512
A_HEADS = 8
A_HEAD_DIM = 64
A_CHUNK = 128
B_WIDTH = 512
PROJ_AB = 2 * A_WIDTH + 3 * B_WIDTH
C_GROUPS = 4
C_WINDOWS = (2, 4, 8, 16)
C_GROUP_DIM = 256
N_EXPERTS = 32
TOP_K = 4
D_FF = 1024
SWIGLU_LIMIT = 7.0
SWIGLU_ALPHA = 1.702
LN_EPS = 1e-5
DEEPNORM_ALPHA = float((2 * DEPTH) ** 0.25)

SUBLANES = 8
LANES = 128
MIX_TILE = 512
CONV_HALO = 8
POOL_HALO = 16
ROUTE_TILE = 256
EXPERT_BLOCK = 256
LOCAL_ROWS = ROUTE_TILE * TOP_K + N_EXPERTS * SUBLANES
LOCAL_GRANULES = LOCAL_ROWS // SUBLANES
VMEM_LIMIT = 56 * 1024 * 1024
EXPERT_GROUP = 2


def _layer_norm(x, g, b=None):
    mu = jnp.mean(x, axis=-1, keepdims=True)
    xc = x - mu
    var = jnp.mean(xc * xc, axis=-1, keepdims=True)
    y = xc * lax.rsqrt(var + LN_EPS) * g
    if b is not None:
        y = y + b
    return y


def _gelu_exact(x):
    return 0.5 * x * (1.0 + lax.erf(x * (1.0 / math.sqrt(2.0))))


def _route_tile(x, rw_ref, rb_ref, upper_ref, lower_ref):
    td = x.shape[0]
    x_hi = x.astype(BF16)
    x_lo = (x - x_hi.astype(F32)).astype(BF16)
    by_hi = jnp.dot(x_hi, rw_ref[...], preferred_element_type=F32)
    by_lo = jnp.dot(x_lo, rw_ref[:, :LANES], preferred_element_type=F32)
    token_major = by_hi[:, :LANES] + (by_hi[:, LANES:] + by_lo)
    logits = token_major.T[:N_EXPERTS, :] + rb_ref[...]
    eio = lax.broadcasted_iota(I32, (N_EXPERTS, td), 0)
    work = logits
    picks, vals = [], []
    for _ in range(TOP_K):
        m = jnp.max(work, axis=0, keepdims=True)
        idx = jnp.min(jnp.where(work == m, eio, N_EXPERTS), axis=0, keepdims=True)
        pick = eio == idx
        picks.append(pick)
        vals.append(m)
        work = jnp.where(pick, -jnp.inf, work)
    ex = [jnp.exp(v - vals[0]) for v in vals]
    den = ex[0] + ex[1] + ex[2] + ex[3]
    gates = [e / den for e in ex]

    onehot = jnp.zeros((N_EXPERTS, td), F32)
    for pick in picks:
        onehot = onehot + pick.astype(F32)
    rank = jnp.dot(onehot.astype(BF16), upper_ref[...], preferred_element_type=F32)
    cnt = jnp.sum(onehot, axis=1, keepdims=True).astype(I32)
    c8 = ((cnt + (SUBLANES - 1)) // SUBLANES) * SUBLANES
    c8b = jnp.broadcast_to(c8.astype(F32), (N_EXPERTS, LANES))
    loff = jnp.dot(lower_ref[...], c8b.astype(BF16), preferred_element_type=F32)[:, 0:1]
    slot = rank + loff
    rows = [jnp.sum(jnp.where(pick, slot, 0.0), axis=0, keepdims=True) for pick in picks]
    pad = jnp.zeros((SUBLANES - TOP_K, td), F32)
    return (jnp.concatenate(rows + [pad], axis=0).astype(I32),
            jnp.concatenate(gates + [pad], axis=0),
            jnp.broadcast_to(c8, (N_EXPERTS, LANES)))


def _finish_mixer_tile(x, mix, lg_ref, lb_ref, route_refs, o_ref, ls_ref, g_ref, c8_ref):
    x1 = _layer_norm(DEEPNORM_ALPHA * x + mix, lg_ref[...], lb_ref[...])
    o_ref[...] = x1
    for h in range(x1.shape[0] // ROUTE_TILE):
        cols = slice(h * ROUTE_TILE, (h + 1) * ROUTE_TILE)
        slots, gates, counts = _route_tile(x1[cols, :], *route_refs)
        ls_ref[:, cols] = slots
        g_ref[:, cols] = gates
        c8_ref[h] = counts


def _split_router_weight(router_w):
    hi = router_w.astype(BF16)
    lo = (router_w - hi.astype(F32)).astype(BF16)
    pad = ((0, 0), (0, LANES - N_EXPERTS))
    return jnp.concatenate([jnp.pad(hi, pad), jnp.pad(lo, pad)], axis=1)


def _even_mixer_kernel(x_ref, win_ref, ng_ref, wcat_ref, bias_ref, cw_ref, wout_ref,
                       lg_ref, lb_ref, rw_ref, rb_ref, upper_ref, lower_ref,
                       o_ref, ls_ref, g_ref, c8_ref, zc_ref):
    i = pl.program_id(0)
    tm = x_ref.shape[0]
    x = x_ref[...]
    z = jnp.dot(x.astype(BF16), win_ref[...], preferred_element_type=F32)
    u = _gelu_exact(z[:, :A_WIDTH])
    v = _gelu_exact(z[:, A_WIDTH:2 * A_WIDTH])
    g_b = z[:, 2 * A_WIDTH:2 * A_WIDTH + B_WIDTH]
    g_c = z[:, 2 * A_WIDTH + B_WIDTH:2 * A_WIDTH + 2 * B_WIDTH]
    x_in = z[:, 2 * A_WIDTH + 2 * B_WIDTH:]

    vb = _layer_norm(v, ng_ref[...]).astype(BF16)
    first_head = lax.broadcasted_iota(I32, (A_CHUNK, LANES), 1) < A_HEAD_DIM
    zero = jnp.zeros((), BF16)
    chunks = []
    for c in range(tm // A_CHUNK):
        slabs = []
        for j in range(A_WIDTH // LANES):
            slab = vb[c * A_CHUNK:(c + 1) * A_CHUNK, j * LANES:(j + 1) * LANES]
            rhs = jnp.concatenate([jnp.where(first_head, slab, zero),
                                   jnp.where(first_head, zero, slab)], axis=0)
            slabs.append(jnp.dot(wcat_ref[j], rhs, preferred_element_type=F32))
        chunks.append(jnp.concatenate(slabs, axis=1) + bias_ref[...])
    y_a = u * jnp.concatenate(chunks, axis=0)

    zz = g_c * x_in

    @pl.when(i == 0)
    def _():
        zc_ref[0:CONV_HALO, :] = jnp.zeros((CONV_HALO, B_WIDTH), F32)

    zc_ref[CONV_HALO:CONV_HALO + tm, :] = zz
    z1 = zc_ref[CONV_HALO - 1:CONV_HALO - 1 + tm, :]
    z2 = zc_ref[CONV_HALO - 2:CONV_HALO - 2 + tm, :]
    cw = cw_ref[...]
    conv = cw[0:1, :] * z2 + cw[1:2, :] * z1 + cw[2:3, :] * zz
    y_b = g_b * conv
    zc_ref[0:CONV_HALO, :] = zc_ref[tm:tm + CONV_HALO, :]

    y = jnp.concatenate([y_a, y_b], axis=1).astype(BF16)
    mix = jnp.dot(y, wout_ref[...], preferred_element_type=F32)
    _finish_mixer_tile(x, mix, lg_ref, lb_ref, (rw_ref, rb_ref, upper_ref, lower_ref),
                       o_ref, ls_ref, g_ref, c8_ref)


def _odd_mixer_kernel(x_ref, win_ref, wgrp_ref, scale_ref, wout_ref, lg_ref, lb_ref,
                      rw_ref, rb_ref, upper_ref, lower_ref,
                      o_ref, ls_ref, g_ref, c8_ref, zh_ref):
    i = pl.program_id(0)
    tm = x_ref.shape[0]
    x = x_ref[...]
    z = jnp.dot(x.astype(BF16), win_ref[...], preferred_element_type=F32)

    @pl.when(i == 0)
    def _():
        zh_ref[0:POOL_HALO, :] = jnp.zeros((POOL_HALO, D_MODEL), F32)

    zh_ref[POOL_HALO:POOL_HALO + tm, :] = z
    pos = i * tm + lax.broadcasted_iota(I32, (tm, 1), 0)
    mixed = []
    for g, win in enumerate(C_WINDOWS):
        cols = slice(g * C_GROUP_DIM, (g + 1) * C_GROUP_DIM)
        acc = zh_ref[:, cols]
        shift = 1
        while shift < win:
            acc = acc + pltpu.roll(acc, shift, axis=0)
            shift *= 2
        cnt = jnp.minimum(pos + 1, win).astype(F32)
        pooled = acc[POOL_HALO:, :] / cnt - z[:, cols]
        mixed.append(jnp.dot(pooled.astype(BF16), wgrp_ref[g], preferred_element_type=F32))
    zh_ref[0:POOL_HALO, :] = zh_ref[tm:tm + POOL_HALO, :]
    y = (jnp.concatenate(mixed, axis=1) * scale_ref[...]).astype(BF16)
    mix = jnp.dot(y, wout_ref[...], preferred_element_type=F32)
    _finish_mixer_tile(x, mix, lg_ref, lb_ref, (rw_ref, rb_ref, upper_ref, lower_ref),
                       o_ref, ls_ref, g_ref, c8_ref)


def _full(shape):
    return pl.BlockSpec(shape, lambda i: (0,) * len(shape))


def _mixer_call(body, name, scratch, x, params, route_params):
    t = x.shape[0]
    halves = MIX_TILE // ROUTE_TILE
    operands = tuple(params) + tuple(route_params)
    return pl.pallas_call(
        body,
        grid=(t // MIX_TILE,),
        in_specs=[pl.BlockSpec((MIX_TILE, D_MODEL), lambda i: (i, 0))]
                 + [_full(p.shape) for p in operands],
        out_specs=[pl.BlockSpec((MIX_TILE, D_MODEL), lambda i: (i, 0)),
                   pl.BlockSpec((SUBLANES, MIX_TILE), lambda i: (0, i)),
                   pl.BlockSpec((SUBLANES, MIX_TILE), lambda i: (0, i)),
                   pl.BlockSpec((halves, N_EXPERTS, LANES), lambda i: (i, 0, 0))],
        out_shape=[jax.ShapeDtypeStruct((t, D_MODEL), F32),
                   jax.ShapeDtypeStruct((SUBLANES, t), I32),
                   jax.ShapeDtypeStruct((SUBLANES, t), F32),
                   jax.ShapeDtypeStruct((t // ROUTE_TILE, N_EXPERTS, LANES), I32)],
        scratch_shapes=[scratch],
        compiler_params=pltpu.CompilerParams(dimension_semantics=("arbitrary",),
                                             vmem_limit_bytes=VMEM_LIMIT),
        name=name,
    )(x, *operands)


def _slot_matches(ls_ref, k):
    s_iota = lax.broadcasted_iota(I32, (LOCAL_ROWS, ROUTE_TILE), 0)
    return s_iota == ls_ref[k:k + 1, :]


def _for_each_granule(tab_s, tile, fn):
    for g in range(LOCAL_GRANULES):
        fn(g * SUBLANES, pl.multiple_of(tab_s[tile * LOCAL_GRANULES + g], SUBLANES))


def _dispatch_kernel(tab_s, tst_s, tn_s,
                     x_ref, ls_ref, xpad_ref, xs_ref, zero_ref, sem, zsem):
    i = pl.program_id(0)
    n = pl.num_programs(0)
    slot = i % 2

    def rows_copy(buf, src_row, dst_row, rows=SUBLANES):
        return pltpu.make_async_copy(xs_ref.at[buf, pl.ds(src_row, rows)],
                                     xpad_ref.at[pl.ds(dst_row, rows)], sem.at[buf])

    def wait_tile(buf):
        rows_copy(buf, 0, 0, LOCAL_ROWS).wait()

    def zero_copy(dst_row):
        return pltpu.make_async_copy(zero_ref, xpad_ref.at[pl.ds(dst_row, SUBLANES)], zsem)

    @pl.when(i == 0)
    def _():
        zero_ref[...] = jnp.zeros(zero_ref.shape, F32)

        def per_expert(e, total):
            def per_granule(g, c):
                zero_copy(pl.multiple_of(tst_s[e] + g * SUBLANES, SUBLANES)).start()
                return c
            lax.fori_loop(0, tn_s[e], per_granule, 0)
            return total + tn_s[e]

        total = lax.fori_loop(0, N_EXPERTS, per_expert, 0)

        def drain(_, c):
            zero_copy(0).wait()
            return c
        lax.fori_loop(0, total, drain, 0)

    match = _slot_matches(ls_ref, 0)
    for k in range(1, TOP_K):
        match = match | _slot_matches(ls_ref, k)
    perm = jnp.where(match, 1.0, 0.0).astype(BF16)
    xs_ref[slot] = jnp.dot(perm, x_ref[...].astype(BF16), preferred_element_type=F32)

    _for_each_granule(tab_s, i, lambda lo, go: rows_copy(slot, lo, go).start())

    @pl.when(i > 0)
    def _():
        wait_tile(1 - slot)

    @pl.when(i == n - 1)
    def _():
        wait_tile(slot)


def _dispatch(tables, x1, lslot, n_rows):
    t = x1.shape[0]
    n_tiles = t // ROUTE_TILE
    grid_spec = pltpu.PrefetchScalarGridSpec(
        num_scalar_prefetch=len(tables),
        grid=(n_tiles,),
        in_specs=[pl.BlockSpec((ROUTE_TILE, D_MODEL), lambda i, *_: (i, 0)),
                  pl.BlockSpec((SUBLANES, ROUTE_TILE), lambda i, *_: (0, i))],
        out_specs=pl.BlockSpec(memory_space=pl.ANY),
        scratch_shapes=[pltpu.VMEM((2, LOCAL_ROWS, D_MODEL), F32),
                        pltpu.VMEM((SUBLANES, D_MODEL), F32),
                        pltpu.SemaphoreType.DMA((2,)),
                        pltpu.SemaphoreType.DMA],
    )
    return pl.pallas_call(
        _dispatch_kernel,
        grid_spec=grid_spec,
        out_shape=jax.ShapeDtypeStruct((n_rows, D_MODEL), F32),
        compiler_params=pltpu.CompilerParams(dimension_semantics=("arbitrary",),
                                             vmem_limit_bytes=VMEM_LIMIT),
        name="dispatch",
    )(*tables, x1, lslot)


def _expert_kernel(pstart_s, nblk_s, xpad_ref, wup_ref, bup_ref, wdn_ref, bdn_ref, ypad_ref,
                   wup_bf, wdn_bf, xbuf, ybuf, in_sem, out_sem):
    e = pl.program_id(0)
    nb = nblk_s[e]
    ng = lax.shift_right_logical(nb + (EXPERT_GROUP - 1), EXPERT_GROUP.bit_length() - 1)
    base = pstart_s[e]
    group_rows = EXPERT_GROUP * EXPERT_BLOCK

    def rows_of(g, n_rows):
        return pl.ds(pl.multiple_of(base + g * group_rows, EXPERT_BLOCK), n_rows)

    def in_copy(g, buf, n_blocks):
        n_rows = n_blocks * EXPERT_BLOCK
        return pltpu.make_async_copy(xpad_ref.at[rows_of(g, n_rows)],
                                     xbuf.at[buf, pl.ds(0, n_rows)], in_sem.at[buf])

    def out_copy(g, buf, n_blocks):
        n_rows = n_blocks * EXPERT_BLOCK
        return pltpu.make_async_copy(ybuf.at[buf, pl.ds(0, n_rows)],
                                     ypad_ref.at[rows_of(g, n_rows)], out_sem.at[buf])

    def blocks_in(g):
        return jnp.minimum(nb - g * EXPERT_GROUP, EXPERT_GROUP)

    def for_group_size(g, fn):
        for n_blocks in range(1, EXPERT_GROUP + 1):
            @pl.when(blocks_in(g) == n_blocks)
            def _():
                fn(n_blocks)

    def start_fetch(g):
        for_group_size(g, lambda n_blocks: in_copy(g, g % 3, n_blocks).start())

    @pl.when(nb > 0)
    def _():
        start_fetch(0)

        @pl.when(ng > 1)
        def _():
            start_fetch(1)

        wup_bf[...] = wup_ref[...].astype(BF16)
        wdn_bf[...] = wdn_ref[...].astype(BF16)

        def body(g, c):
            ibuf = g % 3
            obuf = g % 2

            @pl.when(g + 2 < ng)
            def _():
                start_fetch(g + 2)

            for_group_size(g, lambda n_blocks: in_copy(g, ibuf, n_blocks).wait())

            @pl.when(g >= 2)
            def _():
                out_copy(g - 2, obuf, EXPERT_GROUP).wait()

            def compute_and_store(n_blocks):
                rows = pl.ds(0, n_blocks * EXPERT_BLOCK)
                h = jnp.dot(xbuf[ibuf, rows, :].astype(BF16), wup_bf[...],
                            preferred_element_type=F32) + bup_ref[...]
                gate = jnp.minimum(h[:, :D_FF], SWIGLU_LIMIT)
                lin = jnp.clip(h[:, D_FF:], -SWIGLU_LIMIT, SWIGLU_LIMIT)
                act = (lin + 1.0) * gate * jax.nn.sigmoid(SWIGLU_ALPHA * gate)
                ybuf[obuf, rows, :] = jnp.dot(act.astype(BF16), wdn_bf[...],
                                              preferred_element_type=F32) + bdn_ref[...]
                out_copy(g, obuf, n_blocks).start()

            for_group_size(g, compute_and_store)
            return c

        lax.fori_loop(0, ng, body, 0)

        @pl.when(ng >= 2)
        def _():
            out_copy(ng - 2, ng % 2, EXPERT_GROUP).wait()

        for_group_size(ng - 1, lambda n_blocks: out_copy(ng - 1, (ng - 1) % 2, n_blocks).wait())


def _experts(layer, pstart, nblk, x_pad, w_up, b_up, w_down, b_down):
    def w_map(e, *_):
        return (layer, e, 0, 0)

    grid_spec = pltpu.PrefetchScalarGridSpec(
        num_scalar_prefetch=2,
        grid=(N_EXPERTS,),
        in_specs=[pl.BlockSpec(memory_space=pl.ANY),
                  pl.BlockSpec((None, None, D_MODEL, 2 * D_FF), w_map),
                  pl.BlockSpec((None, None, 1, 2 * D_FF), w_map),
                  pl.BlockSpec((None, None, D_FF, D_MODEL), w_map),
                  pl.BlockSpec((None, None, 1, D_MODEL), w_map)],
        out_specs=pl.BlockSpec(memory_space=pl.ANY),
        scratch_shapes=[pltpu.VMEM((D_MODEL, 2 * D_FF), BF16),
                        pltpu.VMEM((D_FF, D_MODEL), BF16),
                        pltpu.VMEM((3, EXPERT_GROUP * EXPERT_BLOCK, D_MODEL), F32),
                        pltpu.VMEM((2, EXPERT_GROUP * EXPERT_BLOCK, D_MODEL), F32),
                        pltpu.SemaphoreType.DMA((3,)),
                        pltpu.SemaphoreType.DMA((2,))],
    )
    return pl.pallas_call(
        _expert_kernel,
        grid_spec=grid_spec,
        out_shape=jax.ShapeDtypeStruct(x_pad.shape, F32),
        compiler_params=pltpu.CompilerParams(dimension_semantics=("arbitrary",),
                                             vmem_limit_bytes=VMEM_LIMIT),
        name="experts",
    )(pstart, nblk, x_pad, w_up, b_up, w_down, b_down)


def _combine_kernel(tab_s,
                    x_ref, ls_ref, g_ref, lg_ref, lb_ref, ypad_ref, o_ref, yl_ref, sem):
    i = pl.program_id(0)
    n = pl.num_programs(0)
    slot = i % 2

    def rows_copy(buf, src_row, dst_row, rows=SUBLANES):
        return pltpu.make_async_copy(ypad_ref.at[pl.ds(src_row, rows)],
                                     yl_ref.at[buf, pl.ds(dst_row, rows)], sem.at[buf])

    def fetch(tile, buf):
        _for_each_granule(tab_s, tile, lambda lo, go: rows_copy(buf, go, lo).start())

    @pl.when(i == 0)
    def _():
        fetch(0, 0)

    @pl.when(i + 1 < n)
    def _():
        fetch(i + 1, 1 - slot)

    rows_copy(slot, 0, 0, LOCAL_ROWS).wait()

    gated = jnp.zeros((LOCAL_ROWS, ROUTE_TILE), F32)
    for k in range(TOP_K):
        gated = jnp.where(_slot_matches(ls_ref, k), g_ref[k:k + 1, :], gated)
    ffn = jnp.dot(gated.T.astype(BF16), yl_ref[slot].astype(BF16),
                  preferred_element_type=F32)
    o_ref[...] = _layer_norm(DEEPNORM_ALPHA * x_ref[...] + ffn, lg_ref[...], lb_ref[...])


def _combine(tables, x1, lslot, gates, lg, lb, y_pad):
    t = x1.shape[0]
    n_tiles = t // ROUTE_TILE
    grid_spec = pltpu.PrefetchScalarGridSpec(
        num_scalar_prefetch=len(tables),
        grid=(n_tiles,),
        in_specs=[pl.BlockSpec((ROUTE_TILE, D_MODEL), lambda i, *_: (i, 0)),
                  pl.BlockSpec((SUBLANES, ROUTE_TILE), lambda i, *_: (0, i)),
                  pl.BlockSpec((SUBLANES, ROUTE_TILE), lambda i, *_: (0, i)),
                  pl.BlockSpec(lg.shape, lambda i, *_: (0, 0)),
                  pl.BlockSpec(lb.shape, lambda i, *_: (0, 0)),
                  pl.BlockSpec(memory_space=pl.ANY)],
        out_specs=pl.BlockSpec((ROUTE_TILE, D_MODEL), lambda i, *_: (i, 0)),
        scratch_shapes=[pltpu.VMEM((2, LOCAL_ROWS, D_MODEL), F32),
                        pltpu.SemaphoreType.DMA((2,))],
    )
    return pl.pallas_call(
        _combine_kernel,
        grid_spec=grid_spec,
        out_shape=jax.ShapeDtypeStruct((t, D_MODEL), F32),
        compiler_params=pltpu.CompilerParams(dimension_semantics=("arbitrary",),
                                             vmem_limit_bytes=VMEM_LIMIT),
        name="combine",
    )(*tables, x1, lslot, gates, lg, lb, y_pad)


def _grouped_rows_bound(n_tokens):
    n_tiles = n_tokens // ROUTE_TILE
    worst = (n_tokens * TOP_K + n_tiles * N_EXPERTS * (SUBLANES - 1)
             + N_EXPERTS * (EXPERT_BLOCK - 1))
    return -(-worst // EXPERT_BLOCK) * EXPERT_BLOCK


def _routing_tables(c8, spare_base):
    n_tiles = c8.shape[0]
    seg_total = jnp.sum(c8, axis=0)
    padded = (seg_total + EXPERT_BLOCK - 1) // EXPERT_BLOCK * EXPERT_BLOCK
    pstart = jnp.cumsum(padded) - padded
    gstart = pstart[None, :] + jnp.cumsum(c8, axis=0) - c8
    lend = jnp.cumsum(c8, axis=1)
    g_row = jnp.arange(LOCAL_GRANULES, dtype=I32) * SUBLANES
    loff = lend - c8
    row = g_row[None, :, None]
    inside = (loff[:, None, :] <= row) & (row < lend[:, None, :])
    shift = jnp.sum(jnp.where(inside, (gstart - loff)[:, None, :], 0), axis=2)
    granule_row = shift + g_row[None, :]
    occupied = g_row[None, :] < lend[:, -1:]
    spare = (spare_base + (jnp.arange(n_tiles, dtype=I32) % 2)[:, None] * LOCAL_ROWS
             + g_row[None, :])
    dispatch_dst = jnp.where(occupied, granule_row, spare)
    combine_src = jnp.where(occupied, granule_row, g_row[None, :])
    tail_start = pstart + seg_total
    tail_n = (padded - seg_total) // SUBLANES
    as_i32 = lambda a: a.astype(I32)
    return ((as_i32(dispatch_dst.reshape(-1)), as_i32(tail_start), as_i32(tail_n)),
            (as_i32(pstart), as_i32(padded // EXPERT_BLOCK)),
            (as_i32(combine_src.reshape(-1)),))


def _moe(layer, x1, lslot, gates, counts, w_up, b_up, w_down, b_down, lg, lb):
    spare_base = _grouped_rows_bound(x1.shape[0])
    dispatch_tables, expert_tables, combine_tables = _routing_tables(counts[:, :, 0],
                                                                     spare_base)
    x_pad = _dispatch(dispatch_tables, x1, lslot, spare_base + 2 * LOCAL_ROWS)
    y_pad = _experts(layer, *expert_tables, x_pad, w_up, b_up, w_down, b_down)
    return _combine(combine_tables, x1, lslot, gates, lg, lb, y_pad)


def kernel(x, ab_w_in, a_norm_g, a_w_s, a_b_s, b_conv_w, ab_w_out, c_w_in, c_w_grp, c_scale,
           c_w_out, ln_mix_g, ln_mix_b, router_w, router_b, moe_w_up, moe_b_up, moe_w_down,
           moe_b_down, ln_ffn_g, ln_ffn_b):
    bsz, seq, d = x.shape
    assert d == D_MODEL and (bsz * seq) % MIX_TILE == 0 and seq % A_CHUNK == 0
    assert bsz == 1, "token tiles carry conv / pooling state across the sequence"
    h = x.reshape(bsz * seq, d)

    upper = jnp.triu(jnp.ones((ROUTE_TILE, ROUTE_TILE), BF16), k=1)
    lower = jnp.tril(jnp.ones((N_EXPERTS, N_EXPERTS), BF16), k=-1)
    causal = jnp.tril(jnp.ones((A_CHUNK, A_CHUNK), dtype=bool))
    b_up = moe_b_up.reshape(DEPTH, N_EXPERTS, 1, 2 * D_FF)
    b_down = moe_b_down.reshape(DEPTH, N_EXPERTS, 1, D_MODEL)

    for layer in range(DEPTH):
        i = layer // 2
        lg = ln_mix_g[layer].reshape(1, d)
        lb = ln_mix_b[layer].reshape(1, d)
        route_params = (_split_router_weight(router_w[layer]),
                        router_b[layer].reshape(N_EXPERTS, 1), upper, lower)
        if layer % 2 == 0:
            w_s = jnp.where(causal, a_w_s[i], 0.0).astype(BF16)
            wcat = w_s.reshape(A_HEADS // 2, 2, A_CHUNK, A_CHUNK).transpose(0, 2, 1, 3)
            wcat = wcat.reshape(A_HEADS // 2, A_CHUNK, 2 * A_CHUNK)
            bias = jnp.repeat(a_b_s[i].T, A_HEAD_DIM, axis=1)
            params = (ab_w_in[i].astype(BF16), a_norm_g[i].reshape(1, A_WIDTH), wcat, bias,
                      b_conv_w[i], ab_w_out[i].astype(BF16), lg, lb)
            mixed = _mixer_call(_even_mixer_kernel, "even_mixer",
                                pltpu.VMEM((MIX_TILE + CONV_HALO, B_WIDTH), F32),
                                h, params, route_params)
        else:
            params = (c_w_in[i].astype(BF16), c_w_grp[i].astype(BF16),
                      c_scale[i].reshape(1, d), c_w_out[i].astype(BF16), lg, lb)
            mixed = _mixer_call(_odd_mixer_kernel, "odd_mixer",
                                pltpu.VMEM((MIX_TILE + POOL_HALO, D_MODEL), F32),
                                h, params, route_params)
        h = _moe(layer, *mixed, moe_w_up, b_up, moe_w_down, b_down,
                 ln_ffn_g[layer].reshape(1, d), ln_ffn_b[layer].reshape(1, d))
    return h.reshape(bsz, seq, d)
```

```python
import math

import jax
import jax.numpy as jnp
from jax import lax
from jax.experimental import pallas as pl
from jax.experimental.pallas import tpu as pltpu

F32 = jnp.float32
BF16 = jnp.bfloat16
I32 = jnp.int32

D_MODEL = 1024
DEPTH = 4
A_WIDTH = 512
A_HEADS = 8
A_HEAD_DIM = 64
A_CHUNK = 128
B_WIDTH = 512
PROJ_AB = 2 * A_WIDTH + 3 * B_WIDTH
C_GROUPS = 4
C_WINDOWS = (2, 4, 8, 16)
C_GROUP_DIM = 256
N_EXPERTS = 32
TOP_K = 4
D_FF = 1024
SWIGLU_LIMIT = 7.0
SWIGLU_ALPHA = 1.702
LN_EPS = 1e-5
DEEPNORM_ALPHA = float((2 * DEPTH) ** 0.25)

SUBLANES = 8
LANES = 128
MIX_TILE = 1024
CONV_HALO = 8
POOL_HALO = 16
ROUTE_TILE = 256
EXPERT_BLOCK = 256
LOCAL_ROWS = ROUTE_TILE * TOP_K + N_EXPERTS * SUBLANES
LOCAL_GRANULES = LOCAL_ROWS // SUBLANES
FIXED_GRANULES = ROUTE_TILE * TOP_K // SUBLANES
TAIL_CHUNK = 8
TAIL_CHUNKS = (LOCAL_GRANULES - FIXED_GRANULES) // TAIL_CHUNK
VMEM_LIMIT = 56 * 1024 * 1024
EXPERT_GROUP = 2


def _layer_norm(x, g, b=None):
    mu = jnp.mean(x, axis=-1, keepdims=True)
    xc = x - mu
    var = jnp.mean(xc * xc, axis=-1, keepdims=True)
    y = xc * lax.rsqrt(var + LN_EPS) * g
    if b is not None:
        y = y + b
    return y


def _gelu_exact(x):
    return 0.5 * x * (1.0 + lax.erf(x * (1.0 / math.sqrt(2.0))))


def _route_tokens(x, rw_ref, rb_ref):
    td = x.shape[0]
    x_hi = x.astype(BF16)
    x_lo = (x - x_hi.astype(F32)).astype(BF16)
    by_hi = jnp.dot(x_hi, rw_ref[...], preferred_element_type=F32)
    by_lo = jnp.dot(x_lo, rw_ref[:, :LANES], preferred_element_type=F32)
    token_major = by_hi[:, :LANES] + (by_hi[:, LANES:] + by_lo)
    logits = token_major.T[:N_EXPERTS, :] + rb_ref[...]
    eio = lax.broadcasted_iota(I32, (N_EXPERTS, td), 0)
    work = logits
    picks, vals = [], []
    for _ in range(TOP_K):
        m = jnp.max(work, axis=0, keepdims=True)
        idx = jnp.min(jnp.where(work == m, eio, N_EXPERTS), axis=0, keepdims=True)
        pick = eio == idx
        picks.append(pick)
        vals.append(m)
        work = jnp.where(pick, -jnp.inf, work)
    ex = [jnp.exp(v - vals[0]) for v in vals]
    den = ex[0] + ex[1] + ex[2] + ex[3]
    gates = [e / den for e in ex]
    return picks, gates


def _tile_slots(picks, upper_ref, lower_ref):
    td = picks[0].shape[1]
    onehot = jnp.zeros((N_EXPERTS, td), F32)
    for pick in picks:
        onehot = onehot + pick.astype(F32)
    rank = jnp.dot(onehot.astype(BF16), upper_ref[...], preferred_element_type=F32)
    cnt = jnp.sum(onehot, axis=1, keepdims=True).astype(I32)
    c8 = ((cnt + (SUBLANES - 1)) // SUBLANES) * SUBLANES
    c8b = jnp.broadcast_to(c8.astype(F32), (N_EXPERTS, LANES))
    loff = jnp.dot(lower_ref[...], c8b.astype(BF16), preferred_element_type=F32)[:, 0:1]
    slot = rank + loff
    rows = [jnp.sum(jnp.where(pick, slot, 0.0), axis=0, keepdims=True) for pick in picks]
    return rows, jnp.broadcast_to(c8, (N_EXPERTS, LANES))


def _finish_mixer_tile(x, mix, lg_ref, lb_ref, route_refs, o_ref, ls_ref, g_ref, c8_ref):
    rw_ref, rb_ref, upper_ref, lower_ref = route_refs
    x1 = _layer_norm(DEEPNORM_ALPHA * x + mix, lg_ref[...], lb_ref[...])
    o_ref[...] = x1
    tm = x1.shape[0]
    picks, gates = _route_tokens(x1, rw_ref, rb_ref)
    pad = jnp.zeros((SUBLANES - TOP_K, tm), F32)
    g_ref[...] = jnp.concatenate(gates + [pad], axis=0)
    for h in range(tm // ROUTE_TILE):
        cols = slice(h * ROUTE_TILE, (h + 1) * ROUTE_TILE)
        rows, counts = _tile_slots([p[:, cols] for p in picks], upper_ref, lower_ref)
        ls_ref[:, cols] = jnp.concatenate(rows + [pad[:, cols]], axis=0).astype(I32)
        c8_ref[h] = counts


def _split_router_weight(router_w):
    hi = router_w.astype(BF16)
    lo = (router_w - hi.astype(F32)).astype(BF16)
    pad = ((0, 0), (0, LANES - N_EXPERTS))
    return jnp.concatenate([jnp.pad(hi, pad), jnp.pad(lo, pad)], axis=1)


def _even_mixer_kernel(x_ref, win_ref, ng_ref, wcat_ref, bias_ref, cw_ref, wout_ref,
                       lg_ref, lb_ref, rw_ref, rb_ref, upper_ref, lower_ref,
                       o_ref, ls_ref, g_ref, c8_ref, zc_ref):
    i = pl.program_id(0)
    tm = x_ref.shape[0]
    x = x_ref[...]
    z = jnp.dot(x.astype(BF16), win_ref[...], preferred_element_type=F32)
    u = _gelu_exact(z[:, :A_WIDTH])
    v = _gelu_exact(z[:, A_WIDTH:2 * A_WIDTH])
    g_b = z[:, 2 * A_WIDTH:2 * A_WIDTH + B_WIDTH]
    g_c = z[:, 2 * A_WIDTH + B_WIDTH:2 * A_WIDTH + 2 * B_WIDTH]
    x_in = z[:, 2 * A_WIDTH + 2 * B_WIDTH:]

    vb = _layer_norm(v, ng_ref[...]).astype(BF16)
    first_head = lax.broadcasted_iota(I32, (A_CHUNK, LANES), 1) < A_HEAD_DIM
    zero = jnp.zeros((), BF16)
    chunks = []
    for c in range(tm // A_CHUNK):
        slabs = []
        for j in range(A_WIDTH // LANES):
            slab = vb[c * A_CHUNK:(c + 1) * A_CHUNK, j * LANES:(j + 1) * LANES]
            rhs = jnp.concatenate([jnp.where(first_head, slab, zero),
                                   jnp.where(first_head, zero, slab)], axis=0)
            slabs.append(jnp.dot(wcat_ref[j], rhs, preferred_element_type=F32))
        chunks.append(jnp.concatenate(slabs, axis=1) + bias_ref[...])
    y_a = u * jnp.concatenate(chunks, axis=0)

    zz = g_c * x_in

    @pl.when(i == 0)
    def _():
        zc_ref[0:CONV_HALO, :] = jnp.zeros((CONV_HALO, B_WIDTH), F32)

    zc_ref[CONV_HALO:CONV_HALO + tm, :] = zz
    z1 = zc_ref[CONV_HALO - 1:CONV_HALO - 1 + tm, :]
    z2 = zc_ref[CONV_HALO - 2:CONV_HALO - 2 + tm, :]
    cw = cw_ref[...]
    conv = cw[0:1, :] * z2 + cw[1:2, :] * z1 + cw[2:3, :] * zz
    y_b = g_b * conv
    zc_ref[0:CONV_HALO, :] = zc_ref[tm:tm + CONV_HALO, :]

    y = jnp.concatenate([y_a, y_b], axis=1).astype(BF16)
    mix = jnp.dot(y, wout_ref[...], preferred_element_type=F32)
    _finish_mixer_tile(x, mix, lg_ref, lb_ref, (rw_ref, rb_ref, upper_ref, lower_ref),
                       o_ref, ls_ref, g_ref, c8_ref)


def _odd_mixer_kernel(x_ref, win_ref, wgrp_ref, scale_ref, wout_ref, lg_ref, lb_ref,
                      rw_ref, rb_ref, upper_ref, lower_ref,
                      o_ref, ls_ref, g_ref, c8_ref, zh_ref):
    i = pl.program_id(0)
    tm = x_ref.shape[0]
    x = x_ref[...]
    z = jnp.dot(x.astype(BF16), win_ref[...], preferred_element_type=F32)

    @pl.when(i == 0)
    def _():
        zh_ref[0:POOL_HALO, :] = jnp.zeros((POOL_HALO, D_MODEL), F32)

    zh_ref[POOL_HALO:POOL_HALO + tm, :] = z
    pos = i * tm + lax.broadcasted_iota(I32, (tm, 1), 0)
    mixed = []
    for g, win in enumerate(C_WINDOWS):
        cols = slice(g * C_GROUP_DIM, (g + 1) * C_GROUP_DIM)
        acc = zh_ref[:, cols]
        shift = 1
        while shift < win:
            acc = acc + pltpu.roll(acc, shift, axis=0)
            shift *= 2
        cnt = jnp.minimum(pos + 1, win).astype(F32)
        pooled = acc[POOL_HALO:, :] / cnt - z[:, cols]
        mixed.append(jnp.dot(pooled.astype(BF16), wgrp_ref[g], preferred_element_type=F32))
    zh_ref[0:POOL_HALO, :] = zh_ref[tm:tm + POOL_HALO, :]
    y = (jnp.concatenate(mixed, axis=1) * scale_ref[...]).astype(BF16)
    mix = jnp.dot(y, wout_ref[...], preferred_element_type=F32)
    _finish_mixer_tile(x, mix, lg_ref, lb_ref, (rw_ref, rb_ref, upper_ref, lower_ref),
                       o_ref, ls_ref, g_ref, c8_ref)


def _full(shape):
    return pl.BlockSpec(shape, lambda i: (0,) * len(shape))


def _mixer_call(body, name, scratch, x, params, route_params):
    t = x.shape[0]
    halves = MIX_TILE // ROUTE_TILE
    operands = tuple(params) + tuple(route_params)
    return pl.pallas_call(
        body,
        grid=(t // MIX_TILE,),
        in_specs=[pl.BlockSpec((MIX_TILE, D_MODEL), lambda i: (i, 0))]
                 + [_full(p.shape) for p in operands],
        out_specs=[pl.BlockSpec((MIX_TILE, D_MODEL), lambda i: (i, 0)),
                   pl.BlockSpec((SUBLANES, MIX_TILE), lambda i: (0, i)),
                   pl.BlockSpec((SUBLANES, MIX_TILE), lambda i: (0, i)),
                   pl.BlockSpec((halves, N_EXPERTS, LANES), lambda i: (i, 0, 0))],
        out_shape=[jax.ShapeDtypeStruct((t, D_MODEL), F32),
                   jax.ShapeDtypeStruct((SUBLANES, t), I32),
                   jax.ShapeDtypeStruct((SUBLANES, t), F32),
                   jax.ShapeDtypeStruct((t // ROUTE_TILE, N_EXPERTS, LANES), I32)],
        scratch_shapes=[scratch],
        compiler_params=pltpu.CompilerParams(dimension_semantics=("arbitrary",),
                                             vmem_limit_bytes=VMEM_LIMIT),
        name=name,
    )(x, *operands)


def _slot_matches(ls_ref, k):
    s_iota = lax.broadcasted_iota(I32, (LOCAL_ROWS, ROUTE_TILE), 0)
    return s_iota == ls_ref[k:k + 1, :]


def _for_each_granule(tab_s, nch_s, tile, fn):
    def issue(g):
        fn(g * SUBLANES, pl.multiple_of(tab_s[tile * LOCAL_GRANULES + g], SUBLANES))

    for g in range(FIXED_GRANULES):
        issue(g)
    for c in range(TAIL_CHUNKS):
        @pl.when(nch_s[tile] > c)
        def _():
            for g in range(FIXED_GRANULES + c * TAIL_CHUNK, FIXED_GRANULES + (c + 1) * TAIL_CHUNK):
                issue(g)


def _wait_granules(n_chunks, copy_of_rows):
    copy_of_rows(FIXED_GRANULES * SUBLANES).wait()
    for bit in range(TAIL_CHUNKS.bit_length()):
        @pl.when((lax.shift_right_logical(n_chunks, bit) & 1) == 1)
        def _():
            copy_of_rows((TAIL_CHUNK * SUBLANES) << bit).wait()


def _dispatch_kernel(tab_s, nch_s, tst_s, tn_s,
                     x_ref, ls_ref, xpad_ref, xs_ref, zero_ref, sem, zsem):
    i = pl.program_id(0)
    n = pl.num_programs(0)
    slot = i % 2

    def rows_copy(buf, src_row, dst_row, rows=SUBLANES):
        return pltpu.make_async_copy(xs_ref.at[buf, pl.ds(src_row, rows)],
                                     xpad_ref.at[pl.ds(dst_row, rows)], sem.at[buf])

    def wait_tile(tile, buf):
        _wait_granules(nch_s[tile], lambda rows: rows_copy(buf, 0, 0, rows))

    def zero_copy(dst_row):
        return pltpu.make_async_copy(zero_ref, xpad_ref.at[pl.ds(dst_row, SUBLANES)], zsem)

    @pl.when(i == 0)
    def _():
        zero_ref[...] = jnp.zeros(zero_ref.shape, F32)

        def per_expert(e, total):
            def per_granule(g, c):
                zero_copy(pl.multiple_of(tst_s[e] + g * SUBLANES, SUBLANES)).start()
                return c
            lax.fori_loop(0, tn_s[e], per_granule, 0)
            return total + tn_s[e]

        total = lax.fori_loop(0, N_EXPERTS, per_expert, 0)

        def drain(_, c):
            zero_copy(0).wait()
            return c
        lax.fori_loop(0, total, drain, 0)

    match = _slot_matches(ls_ref, 0)
    for k in range(1, TOP_K):
        match = match | _slot_matches(ls_ref, k)
    perm = jnp.where(match, 1.0, 0.0).astype(BF16)
    xs_ref[slot] = jnp.dot(perm, x_ref[...].astype(BF16), preferred_element_type=F32)

    _for_each_granule(tab_s, nch_s, i, lambda lo, go: rows_copy(slot, lo, go).start())

    @pl.when(i > 0)
    def _():
        wait_tile(i - 1, 1 - slot)

    @pl.when(i == n - 1)
    def _():
        wait_tile(i, slot)


def _dispatch(tables, x1, lslot, n_rows):
    t = x1.shape[0]
    n_tiles = t // ROUTE_TILE
    grid_spec = pltpu.PrefetchScalarGridSpec(
        num_scalar_prefetch=len(tables),
        grid=(n_tiles,),
        in_specs=[pl.BlockSpec((ROUTE_TILE, D_MODEL), lambda i, *_: (i, 0)),
                  pl.BlockSpec((SUBLANES, ROUTE_TILE), lambda i, *_: (0, i))],
        out_specs=pl.BlockSpec(memory_space=pl.ANY),
        scratch_shapes=[pltpu.VMEM((2, LOCAL_ROWS, D_MODEL), F32),
                        pltpu.VMEM((SUBLANES, D_MODEL), F32),
                        pltpu.SemaphoreType.DMA((2,)),
                        pltpu.SemaphoreType.DMA],
    )
    return pl.pallas_call(
        _dispatch_kernel,
        grid_spec=grid_spec,
        out_shape=jax.ShapeDtypeStruct((n_rows, D_MODEL), F32),
        compiler_params=pltpu.CompilerParams(dimension_semantics=("arbitrary",),
                                             vmem_limit_bytes=VMEM_LIMIT),
        name="dispatch",
    )(*tables, x1, lslot)


def _expert_kernel(pstart_s, nblk_s, xpad_ref, wup_ref, bup_ref, wdn_ref, bdn_ref, ypad_ref,
                   wup_bf, wdn_bf, xbuf, ybuf, in_sem, out_sem):
    e = pl.program_id(0)
    nb = nblk_s[e]
    ng = lax.shift_right_logical(nb + (EXPERT_GROUP - 1), EXPERT_GROUP.bit_length() - 1)
    base = pstart_s[e]
    group_rows = EXPERT_GROUP * EXPERT_BLOCK

    def rows_of(g, n_rows):
        return pl.ds(pl.multiple_of(base + g * group_rows, EXPERT_BLOCK), n_rows)

    def in_copy(g, buf, n_blocks):
        n_rows = n_blocks * EXPERT_BLOCK
        return pltpu.make_async_copy(xpad_ref.at[rows_of(g, n_rows)],
                                     xbuf.at[buf, pl.ds(0, n_rows)], in_sem.at[buf])

    def out_copy(g, buf, n_blocks):
        n_rows = n_blocks * EXPERT_BLOCK
        return pltpu.make_async_copy(ybuf.at[buf, pl.ds(0, n_rows)],
                                     ypad_ref.at[rows_of(g, n_rows)], out_sem.at[buf])

    def blocks_in(g):
        return jnp.minimum(nb - g * EXPERT_GROUP, EXPERT_GROUP)

    def for_group_size(g, fn):
        for n_blocks in range(1, EXPERT_GROUP + 1):
            @pl.when(blocks_in(g) == n_blocks)
            def _():
                fn(n_blocks)

    def start_fetch(g):
        for_group_size(g, lambda n_blocks: in_copy(g, g % 3, n_blocks).start())

    @pl.when(nb > 0)
    def _():
        start_fetch(0)

        @pl.when(ng > 1)
        def _():
            start_fetch(1)

        wup_bf[...] = wup_ref[...].astype(BF16)
        wdn_bf[...] = wdn_ref[...].astype(BF16)

        def body(g, c):
            ibuf = g % 3
            obuf = g % 2

            @pl.when(g + 2 < ng)
            def _():
                start_fetch(g + 2)

            for_group_size(g, lambda n_blocks: in_copy(g, ibuf, n_blocks).wait())

            @pl.when(g >= 2)
            def _():
                out_copy(g - 2, obuf, EXPERT_GROUP).wait()

            def compute_and_store(n_blocks):
                rows = pl.ds(0, n_blocks * EXPERT_BLOCK)
                h = jnp.dot(xbuf[ibuf, rows, :].astype(BF16), wup_bf[...],
                            preferred_element_type=F32) + bup_ref[...]
                gate = jnp.minimum(h[:, :D_FF], SWIGLU_LIMIT)
                lin = jnp.clip(h[:, D_FF:], -SWIGLU_LIMIT, SWIGLU_LIMIT)
                act = (lin + 1.0) * gate * jax.nn.sigmoid(SWIGLU_ALPHA * gate)
                ybuf[obuf, rows, :] = jnp.dot(act.astype(BF16), wdn_bf[...],
                                              preferred_element_type=F32) + bdn_ref[...]
                out_copy(g, obuf, n_blocks).start()

            for_group_size(g, compute_and_store)
            return c

        lax.fori_loop(0, ng, body, 0)

        @pl.when(ng >= 2)
        def _():
            out_copy(ng - 2, ng % 2, EXPERT_GROUP).wait()

        for_group_size(ng - 1, lambda n_blocks: out_copy(ng - 1, (ng - 1) % 2, n_blocks).wait())


def _experts(layer, pstart, nblk, x_pad, w_up, b_up, w_down, b_down):
    def w_map(e, *_):
        return (layer, e, 0, 0)

    grid_spec = pltpu.PrefetchScalarGridSpec(
        num_scalar_prefetch=2,
        grid=(N_EXPERTS,),
        in_specs=[pl.BlockSpec(memory_space=pl.ANY),
                  pl.BlockSpec((None, None, D_MODEL, 2 * D_FF), w_map),
                  pl.BlockSpec((None, None, 1, 2 * D_FF), w_map),
                  pl.BlockSpec((None, None, D_FF, D_MODEL), w_map),
                  pl.BlockSpec((None, None, 1, D_MODEL), w_map)],
        out_specs=pl.BlockSpec(memory_space=pl.ANY),
        scratch_shapes=[pltpu.VMEM((D_MODEL, 2 * D_FF), BF16),
                        pltpu.VMEM((D_FF, D_MODEL), BF16),
                        pltpu.VMEM((3, EXPERT_GROUP * EXPERT_BLOCK, D_MODEL), F32),
                        pltpu.VMEM((2, EXPERT_GROUP * EXPERT_BLOCK, D_MODEL), F32),
                        pltpu.SemaphoreType.DMA((3,)),
                        pltpu.SemaphoreType.DMA((2,))],
    )
    return pl.pallas_call(
        _expert_kernel,
        grid_spec=grid_spec,
        out_shape=jax.ShapeDtypeStruct(x_pad.shape, F32),
        compiler_params=pltpu.CompilerParams(dimension_semantics=("arbitrary",),
                                             vmem_limit_bytes=VMEM_LIMIT),
        name="experts",
    )(pstart, nblk, x_pad, w_up, b_up, w_down, b_down)


def _combine_kernel(tab_s, nch_s,
                    x_ref, ls_ref, g_ref, lg_ref, lb_ref, ypad_ref, o_ref, yl_ref, sem):
    i = pl.program_id(0)
    n = pl.num_programs(0)
    slot = i % 2

    def rows_copy(buf, src_row, dst_row, rows=SUBLANES):
        return pltpu.make_async_copy(ypad_ref.at[pl.ds(src_row, rows)],
                                     yl_ref.at[buf, pl.ds(dst_row, rows)], sem.at[buf])

    def fetch(tile, buf):
        _for_each_granule(tab_s, nch_s, tile, lambda lo, go: rows_copy(buf, go, lo).start())

    @pl.when(i == 0)
    def _():
        yl_ref[...] = jnp.zeros(yl_ref.shape, F32)
        fetch(0, 0)

    @pl.when(i + 1 < n)
    def _():
        fetch(i + 1, 1 - slot)

    _wait_granules(nch_s[i], lambda rows: rows_copy(slot, 0, 0, rows))

    gated = jnp.zeros((LOCAL_ROWS, ROUTE_TILE), F32)
    for k in range(TOP_K):
        gated = jnp.where(_slot_matches(ls_ref, k), g_ref[k:k + 1, :], gated)
    ffn = jnp.dot(gated.T.astype(BF16), yl_ref[slot].astype(BF16),
                  preferred_element_type=F32)
    o_ref[...] = _layer_norm(DEEPNORM_ALPHA * x_ref[...] + ffn, lg_ref[...], lb_ref[...])


def _combine(tables, x1, lslot, gates, lg, lb, y_pad):
    t = x1.shape[0]
    n_tiles = t // ROUTE_TILE
    grid_spec = pltpu.PrefetchScalarGridSpec(
        num_scalar_prefetch=len(tables),
        grid=(n_tiles,),
        in_specs=[pl.BlockSpec((ROUTE_TILE, D_MODEL), lambda i, *_: (i, 0)),
                  pl.BlockSpec((SUBLANES, ROUTE_TILE), lambda i, *_: (0, i)),
                  pl.BlockSpec((SUBLANES, ROUTE_TILE), lambda i, *_: (0, i)),
                  pl.BlockSpec(lg.shape, lambda i, *_: (0, 0)),
                  pl.BlockSpec(lb.shape, lambda i, *_: (0, 0)),
                  pl.BlockSpec(memory_space=pl.ANY)],
        out_specs=pl.BlockSpec((ROUTE_TILE, D_MODEL), lambda i, *_: (i, 0)),
        scratch_shapes=[pltpu.VMEM((2, LOCAL_ROWS, D_MODEL), F32),
                        pltpu.SemaphoreType.DMA((2,))],
    )
    return pl.pallas_call(
        _combine_kernel,
        grid_spec=grid_spec,
        out_shape=jax.ShapeDtypeStruct((t, D_MODEL), F32),
        compiler_params=pltpu.CompilerParams(dimension_semantics=("arbitrary",),
                                             vmem_limit_bytes=VMEM_LIMIT),
        name="combine",
    )(*tables, x1, lslot, gates, lg, lb, y_pad)


def _grouped_rows_bound(n_tokens):
    n_tiles = n_tokens // ROUTE_TILE
    worst = (n_tokens * TOP_K + n_tiles * N_EXPERTS * (SUBLANES - 1)
             + N_EXPERTS * (EXPERT_BLOCK - 1))
    return -(-worst // EXPERT_BLOCK) * EXPERT_BLOCK


def _routing_tables(c8, spare_base):
    n_tiles = c8.shape[0]
    seg_total = jnp.sum(c8, axis=0)
    padded = (seg_total + EXPERT_BLOCK - 1) // EXPERT_BLOCK * EXPERT_BLOCK
    pstart = jnp.cumsum(padded) - padded
    gstart = pstart[None, :] + jnp.cumsum(c8, axis=0) - c8
    lend = jnp.cumsum(c8, axis=1)
    g_row = jnp.arange(LOCAL_GRANULES, dtype=I32) * SUBLANES
    loff = lend - c8
    row = g_row[None, :, None]
    inside = (loff[:, None, :] <= row) & (row < lend[:, None, :])
    shift = jnp.sum(jnp.where(inside, (gstart - loff)[:, None, :], 0), axis=2)
    granule_row = shift + g_row[None, :]
    occupied = g_row[None, :] < lend[:, -1:]
    spare = (spare_base + (jnp.arange(n_tiles, dtype=I32) % 2)[:, None] * LOCAL_ROWS
             + g_row[None, :])
    dispatch_dst = jnp.where(occupied, granule_row, spare)
    combine_src = jnp.where(occupied, granule_row, g_row[None, :])
    tail_start = pstart + seg_total
    tail_n = (padded - seg_total) // SUBLANES
    n_chunks = (lend[:, -1] // SUBLANES - FIXED_GRANULES + (TAIL_CHUNK - 1)) // TAIL_CHUNK
    as_i32 = lambda a: a.astype(I32)
    return ((as_i32(dispatch_dst.reshape(-1)), as_i32(n_chunks),
             as_i32(tail_start), as_i32(tail_n)),
            (as_i32(pstart), as_i32(padded // EXPERT_BLOCK)),
            (as_i32(combine_src.reshape(-1)), as_i32(n_chunks)))


def _moe(layer, x1, lslot, gates, counts, w_up, b_up, w_down, b_down, lg, lb):
    spare_base = _grouped_rows_bound(x1.shape[0])
    dispatch_tables, expert_tables, combine_tables = _routing_tables(counts[:, :, 0],
                                                                     spare_base)
    x_pad = _dispatch(dispatch_tables, x1, lslot, spare_base + 2 * LOCAL_ROWS)
    y_pad = _experts(layer, *expert_tables, x_pad, w_up, b_up, w_down, b_down)
    return _combine(combine_tables, x1, lslot, gates, lg, lb, y_pad)


def kernel(x, ab_w_in, a_norm_g, a_w_s, a_b_s, b_conv_w, ab_w_out, c_w_in, c_w_grp, c_scale,
           c_w_out, ln_mix_g, ln_mix_b, router_w, router_b, moe_w_up, moe_b_up, moe_w_down,
           moe_b_down, ln_ffn_g, ln_ffn_b):
    bsz, seq, d = x.shape
    assert d == D_MODEL and (bsz * seq) % MIX_TILE == 0 and seq % A_CHUNK == 0
    assert bsz == 1, "token tiles carry conv / pooling state across the sequence"
    h = x.reshape(bsz * seq, d)

    upper = jnp.triu(jnp.ones((ROUTE_TILE, ROUTE_TILE), BF16), k=1)
    lower = jnp.tril(jnp.ones((N_EXPERTS, N_EXPERTS), BF16), k=-1)
    causal = jnp.tril(jnp.ones((A_CHUNK, A_CHUNK), dtype=bool))
    b_up = moe_b_up.reshape(DEPTH, N_EXPERTS, 1, 2 * D_FF)
    b_down = moe_b_down.reshape(DEPTH, N_EXPERTS, 1, D_MODEL)

    for layer in range(DEPTH):
        i = layer // 2
        lg = ln_mix_g[layer].reshape(1, d)
        lb = ln_mix_b[layer].reshape(1, d)
        route_params = (_split_router_weight(router_w[layer]),
                        router_b[layer].reshape(N_EXPERTS, 1), upper, lower)
        if layer % 2 == 0:
            w_s = jnp.where(causal, a_w_s[i], 0.0).astype(BF16)
            wcat = w_s.reshape(A_HEADS // 2, 2, A_CHUNK, A_CHUNK).transpose(0, 2, 1, 3)
            wcat = wcat.reshape(A_HEADS // 2, A_CHUNK, 2 * A_CHUNK)
            bias = jnp.repeat(a_b_s[i].T, A_HEAD_DIM, axis=1)
            params = (ab_w_in[i].astype(BF16), a_norm_g[i].reshape(1, A_WIDTH), wcat, bias,
                      b_conv_w[i], ab_w_out[i].astype(BF16), lg, lb)
            mixed = _mixer_call(_even_mixer_kernel, "even_mixer",
                                pltpu.VMEM((MIX_TILE + CONV_HALO, B_WIDTH), F32),
                                h, params, route_params)
        else:
            params = (c_w_in[i].astype(BF16), c_w_grp[i].astype(BF16),
                      c_scale[i].reshape(1, d), c_w_out[i].astype(BF16), lg, lb)
            mixed = _mixer_call(_odd_mixer_kernel, "odd_mixer",
                                pltpu.VMEM((MIX_TILE + POOL_HALO, D_MODEL), F32),
                                h, params, route_params)
        h = _moe(layer, *mixed, moe_w_up, b_up, moe_w_down, b_down,
                 ln_ffn_g[layer].reshape(1, d), ln_ffn_b[layer].reshape(1, d))
    return h.reshape(bsz, seq, d)
```

```python
import math

import jax
import jax.numpy as jnp
from jax import lax
from jax.experimental import pallas as pl
from jax.experimental.pallas import tpu as pltpu

F32 = jnp.float32
BF16 = jnp.bfloat16
I32 = jnp.int32

D_MODEL = 1024
DEPTH = 4
A_WIDTH = 512
A_HEADS = 8
A_HEAD_DIM = 64
A_CHUNK = 128
B_WIDTH = 512
PROJ_AB = 2 * A_WIDTH + 3 * B_WIDTH
C_GROUPS = 4
C_WINDOWS = (2, 4, 8, 16)
C_GROUP_DIM = 256
N_EXPERTS = 32
TOP_K = 4
D_FF = 1024
SWIGLU_LIMIT = 7.0
SWIGLU_ALPHA = 1.702
LN_EPS = 1e-5
DEEPNORM_ALPHA = float((2 * DEPTH) ** 0.25)

SUBLANES = 8
LANES = 128
MIX_TILE = 1024
CONV_HALO = 8
POOL_HALO = 16
ROUTE_TILE = 256
EXPERT_BLOCK = 128
LOCAL_ROWS = ROUTE_TILE * TOP_K + N_EXPERTS * SUBLANES
LOCAL_GRANULES = LOCAL_ROWS // SUBLANES
FIXED_GRANULES = ROUTE_TILE * TOP_K // SUBLANES
TAIL_CHUNK = 8
TAIL_CHUNKS = (LOCAL_GRANULES - FIXED_GRANULES) // TAIL_CHUNK
VMEM_LIMIT = 56 * 1024 * 1024
EXPERT_GROUP = 4


def _layer_norm(x, g, b=None):
    mu = jnp.mean(x, axis=-1, keepdims=True)
    xc = x - mu
    var = jnp.mean(xc * xc, axis=-1, keepdims=True)
    y = xc * lax.rsqrt(var + LN_EPS) * g
    if b is not None:
        y = y + b
    return y


def _gelu_exact(x):
    return 0.5 * x * (1.0 + lax.erf(x * (1.0 / math.sqrt(2.0))))


def _route_tokens(x, rw_ref, rb_ref):
    td = x.shape[0]
    x_hi = x.astype(BF16)
    x_lo = (x - x_hi.astype(F32)).astype(BF16)
    by_hi = jnp.dot(x_hi, rw_ref[...], preferred_element_type=F32)
    by_lo = jnp.dot(x_lo, rw_ref[:, :LANES], preferred_element_type=F32)
    token_major = by_hi[:, :LANES] + (by_hi[:, LANES:] + by_lo)
    logits = token_major.T[:N_EXPERTS, :] + rb_ref[...]
    eio = lax.broadcasted_iota(I32, (N_EXPERTS, td), 0)
    work = logits
    picks, vals = [], []
    for _ in range(TOP_K):
        m = jnp.max(work, axis=0, keepdims=True)
        idx = jnp.min(jnp.where(work == m, eio, N_EXPERTS), axis=0, keepdims=True)
        pick = eio == idx
        picks.append(pick)
        vals.append(m)
        work = jnp.where(pick, -jnp.inf, work)
    ex = [jnp.exp(v - vals[0]) for v in vals]
    den = ex[0] + ex[1] + ex[2] + ex[3]
    gates = [e / den for e in ex]
    return picks, gates


def _tile_slots(picks, upper_ref, lower_ref):
    td = picks[0].shape[1]
    onehot = jnp.zeros((N_EXPERTS, td), F32)
    for pick in picks:
        onehot = onehot + pick.astype(F32)
    rank = jnp.dot(onehot.astype(BF16), upper_ref[...], preferred_element_type=F32)
    cnt = jnp.sum(onehot, axis=1, keepdims=True).astype(I32)
    c8 = ((cnt + (SUBLANES - 1)) // SUBLANES) * SUBLANES
    c8b = jnp.broadcast_to(c8.astype(F32), (N_EXPERTS, LANES))
    loff = jnp.dot(lower_ref[...], c8b.astype(BF16), preferred_element_type=F32)[:, 0:1]
    slot = rank + loff
    rows = [jnp.sum(jnp.where(pick, slot, 0.0), axis=0, keepdims=True) for pick in picks]
    return rows, jnp.broadcast_to(c8, (N_EXPERTS, LANES))


def _finish_mixer_tile(x, mix, lg_ref, lb_ref, route_refs, o_ref, ls_ref, g_ref, c8_ref):
    rw_ref, rb_ref, upper_ref, lower_ref = route_refs
    x1 = _layer_norm(DEEPNORM_ALPHA * x + mix, lg_ref[...], lb_ref[...])
    o_ref[...] = x1
    tm = x1.shape[0]
    picks, gates = _route_tokens(x1, rw_ref, rb_ref)
    pad = jnp.zeros((SUBLANES - TOP_K, tm), F32)
    g_ref[...] = jnp.concatenate(gates + [pad], axis=0)
    for h in range(tm // ROUTE_TILE):
        cols = slice(h * ROUTE_TILE, (h + 1) * ROUTE_TILE)
        rows, counts = _tile_slots([p[:, cols] for p in picks], upper_ref, lower_ref)
        ls_ref[:, cols] = jnp.concatenate(rows + [pad[:, cols]], axis=0).astype(I32)
        c8_ref[h] = counts


def _split_router_weight(router_w):
    hi = router_w.astype(BF16)
    lo = (router_w - hi.astype(F32)).astype(BF16)
    pad = ((0, 0), (0, LANES - N_EXPERTS))
    return jnp.concatenate([jnp.pad(hi, pad), jnp.pad(lo, pad)], axis=1)


def _even_mixer_kernel(x_ref, win_ref, ng_ref, wcat_ref, bias_ref, cw_ref, wout_ref,
                       lg_ref, lb_ref, rw_ref, rb_ref, upper_ref, lower_ref,
                       o_ref, ls_ref, g_ref, c8_ref, zc_ref):
    i = pl.program_id(0)
    tm = x_ref.shape[0]
    x = x_ref[...]
    z = jnp.dot(x.astype(BF16), win_ref[...], preferred_element_type=F32)
    u = _gelu_exact(z[:, :A_WIDTH])
    v = _gelu_exact(z[:, A_WIDTH:2 * A_WIDTH])
    g_b = z[:, 2 * A_WIDTH:2 * A_WIDTH + B_WIDTH]
    g_c = z[:, 2 * A_WIDTH + B_WIDTH:2 * A_WIDTH + 2 * B_WIDTH]
    x_in = z[:, 2 * A_WIDTH + 2 * B_WIDTH:]

    vb = _layer_norm(v, ng_ref[...]).astype(BF16)
    first_head = lax.broadcasted_iota(I32, (A_CHUNK, LANES), 1) < A_HEAD_DIM
    zero = jnp.zeros((), BF16)
    chunks = []
    for c in range(tm // A_CHUNK):
        slabs = []
        for j in range(A_WIDTH // LANES):
            slab = vb[c * A_CHUNK:(c + 1) * A_CHUNK, j * LANES:(j + 1) * LANES]
            rhs = jnp.concatenate([jnp.where(first_head, slab, zero),
                                   jnp.where(first_head, zero, slab)], axis=0)
            slabs.append(jnp.dot(wcat_ref[j], rhs, preferred_element_type=F32))
        chunks.append(jnp.concatenate(slabs, axis=1) + bias_ref[...])
    y_a = u * jnp.concatenate(chunks, axis=0)

    zz = g_c * x_in

    @pl.when(i == 0)
    def _():
        zc_ref[0:CONV_HALO, :] = jnp.zeros((CONV_HALO, B_WIDTH), F32)

    zc_ref[CONV_HALO:CONV_HALO + tm, :] = zz
    z1 = zc_ref[CONV_HALO - 1:CONV_HALO - 1 + tm, :]
    z2 = zc_ref[CONV_HALO - 2:CONV_HALO - 2 + tm, :]
    cw = cw_ref[...]
    conv = cw[0:1, :] * z2 + cw[1:2, :] * z1 + cw[2:3, :] * zz
    y_b = g_b * conv
    zc_ref[0:CONV_HALO, :] = zc_ref[tm:tm + CONV_HALO, :]

    y = jnp.concatenate([y_a, y_b], axis=1).astype(BF16)
    mix = jnp.dot(y, wout_ref[...], preferred_element_type=F32)
    _finish_mixer_tile(x, mix, lg_ref, lb_ref, (rw_ref, rb_ref, upper_ref, lower_ref),
                       o_ref, ls_ref, g_ref, c8_ref)


def _odd_mixer_kernel(x_ref, win_ref, wgrp_ref, scale_ref, wout_ref, lg_ref, lb_ref,
                      rw_ref, rb_ref, upper_ref, lower_ref,
                      o_ref, ls_ref, g_ref, c8_ref, zh_ref):
    i = pl.program_id(0)
    tm = x_ref.shape[0]
    x = x_ref[...]
    z = jnp.dot(x.astype(BF16), win_ref[...], preferred_element_type=F32)

    @pl.when(i == 0)
    def _():
        zh_ref[0:POOL_HALO, :] = jnp.zeros((POOL_HALO, D_MODEL), F32)

    zh_ref[POOL_HALO:POOL_HALO + tm, :] = z
    pos = i * tm + lax.broadcasted_iota(I32, (tm, 1), 0)
    mixed = []
    for g, win in enumerate(C_WINDOWS):
        cols = slice(g * C_GROUP_DIM, (g + 1) * C_GROUP_DIM)
        acc = zh_ref[:, cols]
        shift = 1
        while shift < win:
            acc = acc + pltpu.roll(acc, shift, axis=0)
            shift *= 2
        cnt = jnp.minimum(pos + 1, win).astype(F32)
        pooled = acc[POOL_HALO:, :] / cnt - z[:, cols]
        mixed.append(jnp.dot(pooled.astype(BF16), wgrp_ref[g], preferred_element_type=F32))
    zh_ref[0:POOL_HALO, :] = zh_ref[tm:tm + POOL_HALO, :]
    y = (jnp.concatenate(mixed, axis=1) * scale_ref[...]).astype(BF16)
    mix = jnp.dot(y, wout_ref[...], preferred_element_type=F32)
    _finish_mixer_tile(x, mix, lg_ref, lb_ref, (rw_ref, rb_ref, upper_ref, lower_ref),
                       o_ref, ls_ref, g_ref, c8_ref)


def _full(shape):
    return pl.BlockSpec(shape, lambda i: (0,) * len(shape))


def _mixer_call(body, name, scratch, x, params, route_params):
    t = x.shape[0]
    halves = MIX_TILE // ROUTE_TILE
    operands = tuple(params) + tuple(route_params)
    return pl.pallas_call(
        body,
        grid=(t // MIX_TILE,),
        in_specs=[pl.BlockSpec((MIX_TILE, D_MODEL), lambda i: (i, 0))]
                 + [_full(p.shape) for p in operands],
        out_specs=[pl.BlockSpec((MIX_TILE, D_MODEL), lambda i: (i, 0)),
                   pl.BlockSpec((SUBLANES, MIX_TILE), lambda i: (0, i)),
                   pl.BlockSpec((SUBLANES, MIX_TILE), lambda i: (0, i)),
                   pl.BlockSpec((halves, N_EXPERTS, LANES), lambda i: (i, 0, 0))],
        out_shape=[jax.ShapeDtypeStruct((t, D_MODEL), F32),
                   jax.ShapeDtypeStruct((SUBLANES, t), I32),
                   jax.ShapeDtypeStruct((SUBLANES, t), F32),
                   jax.ShapeDtypeStruct((t // ROUTE_TILE, N_EXPERTS, LANES), I32)],
        scratch_shapes=[scratch],
        compiler_params=pltpu.CompilerParams(dimension_semantics=("arbitrary",),
                                             vmem_limit_bytes=VMEM_LIMIT),
        name=name,
    )(x, *operands)


def _slot_matches(ls_ref, k):
    s_iota = lax.broadcasted_iota(I32, (LOCAL_ROWS, ROUTE_TILE), 0)
    return s_iota == ls_ref[k:k + 1, :]


def _for_each_granule(tab_s, nch_s, tile, fn):
    def issue(g):
        fn(g * SUBLANES, pl.multiple_of(tab_s[tile * LOCAL_GRANULES + g], SUBLANES))

    for g in range(FIXED_GRANULES):
        issue(g)
    for c in range(TAIL_CHUNKS):
        @pl.when(nch_s[tile] > c)
        def _():
            for g in range(FIXED_GRANULES + c * TAIL_CHUNK, FIXED_GRANULES + (c + 1) * TAIL_CHUNK):
                issue(g)


def _wait_granules(n_chunks, copy_of_rows):
    copy_of_rows(FIXED_GRANULES * SUBLANES).wait()
    for bit in range(TAIL_CHUNKS.bit_length()):
        @pl.when((lax.shift_right_logical(n_chunks, bit) & 1) == 1)
        def _():
            copy_of_rows((TAIL_CHUNK * SUBLANES) << bit).wait()


def _dispatch_kernel(tab_s, nch_s, tst_s, tn_s,
                     x_ref, ls_ref, xpad_ref, xs_ref, zero_ref, sem, zsem):
    i = pl.program_id(0)
    n = pl.num_programs(0)
    slot = i % 2

    def rows_copy(buf, src_row, dst_row, rows=SUBLANES):
        return pltpu.make_async_copy(xs_ref.at[buf, pl.ds(src_row, rows)],
                                     xpad_ref.at[pl.ds(dst_row, rows)], sem.at[buf])

    def wait_tile(tile, buf):
        _wait_granules(nch_s[tile], lambda rows: rows_copy(buf, 0, 0, rows))

    def zero_copy(dst_row):
        return pltpu.make_async_copy(zero_ref, xpad_ref.at[pl.ds(dst_row, SUBLANES)], zsem)

    @pl.when(i == 0)
    def _():
        zero_ref[...] = jnp.zeros(zero_ref.shape, F32)

        def per_expert(e, total):
            def per_granule(g, c):
                zero_copy(pl.multiple_of(tst_s[e] + g * SUBLANES, SUBLANES)).start()
                return c
            lax.fori_loop(0, tn_s[e], per_granule, 0)
            return total + tn_s[e]

        total = lax.fori_loop(0, N_EXPERTS, per_expert, 0)

        def drain(_, c):
            zero_copy(0).wait()
            return c
        lax.fori_loop(0, total, drain, 0)

    match = _slot_matches(ls_ref, 0)
    for k in range(1, TOP_K):
        match = match | _slot_matches(ls_ref, k)
    perm = jnp.where(match, 1.0, 0.0).astype(BF16)
    xs_ref[slot] = jnp.dot(perm, x_ref[...].astype(BF16), preferred_element_type=F32)

    _for_each_granule(tab_s, nch_s, i, lambda lo, go: rows_copy(slot, lo, go).start())

    @pl.when(i > 0)
    def _():
        wait_tile(i - 1, 1 - slot)

    @pl.when(i == n - 1)
    def _():
        wait_tile(i, slot)


def _dispatch(tables, x1, lslot, n_rows):
    t = x1.shape[0]
    n_tiles = t // ROUTE_TILE
    grid_spec = pltpu.PrefetchScalarGridSpec(
        num_scalar_prefetch=len(tables),
        grid=(n_tiles,),
        in_specs=[pl.BlockSpec((ROUTE_TILE, D_MODEL), lambda i, *_: (i, 0)),
                  pl.BlockSpec((SUBLANES, ROUTE_TILE), lambda i, *_: (0, i))],
        out_specs=pl.BlockSpec(memory_space=pl.ANY),
        scratch_shapes=[pltpu.VMEM((2, LOCAL_ROWS, D_MODEL), F32),
                        pltpu.VMEM((SUBLANES, D_MODEL), F32),
                        pltpu.SemaphoreType.DMA((2,)),
                        pltpu.SemaphoreType.DMA],
    )
    return pl.pallas_call(
        _dispatch_kernel,
        grid_spec=grid_spec,
        out_shape=jax.ShapeDtypeStruct((n_rows, D_MODEL), F32),
        compiler_params=pltpu.CompilerParams(dimension_semantics=("arbitrary",),
                                             vmem_limit_bytes=VMEM_LIMIT),
        name="dispatch",
    )(*tables, x1, lslot)


def _expert_kernel(pstart_s, nblk_s, xpad_ref, wup_ref, bup_ref, wdn_ref, bdn_ref, ypad_ref,
                   wup_bf, wdn_bf, xbuf, ybuf, in_sem, out_sem):
    e = pl.program_id(0)
    nb = nblk_s[e]
    ng = lax.shift_right_logical(nb + (EXPERT_GROUP - 1), EXPERT_GROUP.bit_length() - 1)
    base = pstart_s[e]
    group_rows = EXPERT_GROUP * EXPERT_BLOCK

    def rows_of(g, n_rows):
        return pl.ds(pl.multiple_of(base + g * group_rows, EXPERT_BLOCK), n_rows)

    def in_copy(g, buf, n_blocks):
        n_rows = n_blocks * EXPERT_BLOCK
        return pltpu.make_async_copy(xpad_ref.at[rows_of(g, n_rows)],
                                     xbuf.at[buf, pl.ds(0, n_rows)], in_sem.at[buf])

    def out_copy(g, buf, n_blocks):
        n_rows = n_blocks * EXPERT_BLOCK
        return pltpu.make_async_copy(ybuf.at[buf, pl.ds(0, n_rows)],
                                     ypad_ref.at[rows_of(g, n_rows)], out_sem.at[buf])

    def blocks_in(g):
        return jnp.minimum(nb - g * EXPERT_GROUP, EXPERT_GROUP)

    def for_group_size(g, fn):
        for n_blocks in range(1, EXPERT_GROUP + 1):
            @pl.when(blocks_in(g) == n_blocks)
            def _():
                fn(n_blocks)

    def start_fetch(g):
        for_group_size(g, lambda n_blocks: in_copy(g, g % 3, n_blocks).start())

    @pl.when(nb > 0)
    def _():
        start_fetch(0)

        @pl.when(ng > 1)
        def _():
            start_fetch(1)

        wup_bf[...] = wup_ref[...].astype(BF16)
        wdn_bf[...] = wdn_ref[...].astype(BF16)

        def body(g, c):
            ibuf = g % 3
            obuf = g % 2

            @pl.when(g + 2 < ng)
            def _():
                start_fetch(g + 2)

            for_group_size(g, lambda n_blocks: in_copy(g, ibuf, n_blocks).wait())

            @pl.when(g >= 2)
            def _():
                out_copy(g - 2, obuf, EXPERT_GROUP).wait()

            def compute_and_store(n_blocks):
                rows = pl.ds(0, n_blocks * EXPERT_BLOCK)
                h = jnp.dot(xbuf[ibuf, rows, :].astype(BF16), wup_bf[...],
                            preferred_element_type=F32) + bup_ref[...]
                gate = jnp.minimum(h[:, :D_FF], SWIGLU_LIMIT)
                lin = jnp.clip(h[:, D_FF:], -SWIGLU_LIMIT, SWIGLU_LIMIT)
                act = (lin + 1.0) * gate * jax.nn.sigmoid(SWIGLU_ALPHA * gate)
                ybuf[obuf, rows, :] = jnp.dot(act.astype(BF16), wdn_bf[...],
                                              preferred_element_type=F32) + bdn_ref[...]
                out_copy(g, obuf, n_blocks).start()

            for_group_size(g, compute_and_store)
            return c

        lax.fori_loop(0, ng, body, 0)

        @pl.when(ng >= 2)
        def _():
            out_copy(ng - 2, ng % 2, EXPERT_GROUP).wait()

        for_group_size(ng - 1, lambda n_blocks: out_copy(ng - 1, (ng - 1) % 2, n_blocks).wait())


def _experts(layer, pstart, nblk, x_pad, w_up, b_up, w_down, b_down):
    def w_map(e, *_):
        return (layer, e, 0, 0)

    grid_spec = pltpu.PrefetchScalarGridSpec(
        num_scalar_prefetch=2,
        grid=(N_EXPERTS,),
        in_specs=[pl.BlockSpec(memory_space=pl.ANY),
                  pl.BlockSpec((None, None, D_MODEL, 2 * D_FF), w_map),
                  pl.BlockSpec((None, None, 1, 2 * D_FF), w_map),
                  pl.BlockSpec((None, None, D_FF, D_MODEL), w_map),
                  pl.BlockSpec((None, None, 1, D_MODEL), w_map)],
        out_specs=pl.BlockSpec(memory_space=pl.ANY),
        scratch_shapes=[pltpu.VMEM((D_MODEL, 2 * D_FF), BF16),
                        pltpu.VMEM((D_FF, D_MODEL), BF16),
                        pltpu.VMEM((3, EXPERT_GROUP * EXPERT_BLOCK, D_MODEL), F32),
                        pltpu.VMEM((2, EXPERT_GROUP * EXPERT_BLOCK, D_MODEL), F32),
                        pltpu.SemaphoreType.DMA((3,)),
                        pltpu.SemaphoreType.DMA((2,))],
    )
    return pl.pallas_call(
        _expert_kernel,
        grid_spec=grid_spec,
        out_shape=jax.ShapeDtypeStruct(x_pad.shape, F32),
        compiler_params=pltpu.CompilerParams(dimension_semantics=("arbitrary",),
                                             vmem_limit_bytes=VMEM_LIMIT),
        name="experts",
    )(pstart, nblk, x_pad, w_up, b_up, w_down, b_down)


def _combine_kernel(tab_s, nch_s,
                    x_ref, ls_ref, g_ref, lg_ref, lb_ref, ypad_ref, o_ref, yl_ref, sem):
    i = pl.program_id(0)
    n = pl.num_programs(0)
    slot = i % 2

    def rows_copy(buf, src_row, dst_row, rows=SUBLANES):
        return pltpu.make_async_copy(ypad_ref.at[pl.ds(src_row, rows)],
                                     yl_ref.at[buf, pl.ds(dst_row, rows)], sem.at[buf])

    def fetch(tile, buf):
        _for_each_granule(tab_s, nch_s, tile, lambda lo, go: rows_copy(buf, go, lo).start())

    @pl.when(i == 0)
    def _():
        yl_ref[...] = jnp.zeros(yl_ref.shape, F32)
        fetch(0, 0)

    @pl.when(i + 1 < n)
    def _():
        fetch(i + 1, 1 - slot)

    _wait_granules(nch_s[i], lambda rows: rows_copy(slot, 0, 0, rows))

    gated = jnp.zeros((LOCAL_ROWS, ROUTE_TILE), F32)
    for k in range(TOP_K):
        gated = jnp.where(_slot_matches(ls_ref, k), g_ref[k:k + 1, :], gated)
    ffn = jnp.dot(gated.T.astype(BF16), yl_ref[slot].astype(BF16),
                  preferred_element_type=F32)
    o_ref[...] = _layer_norm(DEEPNORM_ALPHA * x_ref[...] + ffn, lg_ref[...], lb_ref[...])


def _combine(tables, x1, lslot, gates, lg, lb, y_pad):
    t = x1.shape[0]
    n_tiles = t // ROUTE_TILE
    grid_spec = pltpu.PrefetchScalarGridSpec(
        num_scalar_prefetch=len(tables),
        grid=(n_tiles,),
        in_specs=[pl.BlockSpec((ROUTE_TILE, D_MODEL), lambda i, *_: (i, 0)),
                  pl.BlockSpec((SUBLANES, ROUTE_TILE), lambda i, *_: (0, i)),
                  pl.BlockSpec((SUBLANES, ROUTE_TILE), lambda i, *_: (0, i)),
                  pl.BlockSpec(lg.shape, lambda i, *_: (0, 0)),
                  pl.BlockSpec(lb.shape, lambda i, *_: (0, 0)),
                  pl.BlockSpec(memory_space=pl.ANY)],
        out_specs=pl.BlockSpec((ROUTE_TILE, D_MODEL), lambda i, *_: (i, 0)),
        scratch_shapes=[pltpu.VMEM((2, LOCAL_ROWS, D_MODEL), F32),
                        pltpu.SemaphoreType.DMA((2,))],
    )
    return pl.pallas_call(
        _combine_kernel,
        grid_spec=grid_spec,
        out_shape=jax.ShapeDtypeStruct((t, D_MODEL), F32),
        compiler_params=pltpu.CompilerParams(dimension_semantics=("arbitrary",),
                                             vmem_limit_bytes=VMEM_LIMIT),
        name="combine",
    )(*tables, x1, lslot, gates, lg, lb, y_pad)


def _grouped_rows_bound(n_tokens):
    n_tiles = n_tokens // ROUTE_TILE
    worst = (n_tokens * TOP_K + n_tiles * N_EXPERTS * (SUBLANES - 1)
             + N_EXPERTS * (EXPERT_BLOCK - 1))
    return -(-worst // EXPERT_BLOCK) * EXPERT_BLOCK


def _routing_tables(c8, spare_base):
    n_tiles = c8.shape[0]
    seg_total = jnp.sum(c8, axis=0)
    padded = (seg_total + EXPERT_BLOCK - 1) // EXPERT_BLOCK * EXPERT_BLOCK
    pstart = jnp.cumsum(padded) - padded
    gstart = pstart[None, :] + jnp.cumsum(c8, axis=0) - c8
    lend = jnp.cumsum(c8, axis=1)
    g_row = jnp.arange(LOCAL_GRANULES, dtype=I32) * SUBLANES
    loff = lend - c8
    row = g_row[None, :, None]
    inside = (loff[:, None, :] <= row) & (row < lend[:, None, :])
    shift = jnp.sum(jnp.where(inside, (gstart - loff)[:, None, :], 0), axis=2)
    granule_row = shift + g_row[None, :]
    occupied = g_row[None, :] < lend[:, -1:]
    spare = (spare_base + (jnp.arange(n_tiles, dtype=I32) % 2)[:, None] * LOCAL_ROWS
             + g_row[None, :])
    dispatch_dst = jnp.where(occupied, granule_row, spare)
    combine_src = jnp.where(occupied, granule_row, g_row[None, :])
    tail_start = pstart + seg_total
    tail_n = (padded - seg_total) // SUBLANES
    n_chunks = (lend[:, -1] // SUBLANES - FIXED_GRANULES + (TAIL_CHUNK - 1)) // TAIL_CHUNK
    as_i32 = lambda a: a.astype(I32)
    return ((as_i32(dispatch_dst.reshape(-1)), as_i32(n_chunks),
             as_i32(tail_start), as_i32(tail_n)),
            (as_i32(pstart), as_i32(padded // EXPERT_BLOCK)),
            (as_i32(combine_src.reshape(-1)), as_i32(n_chunks)))


def _moe(layer, x1, lslot, gates, counts, w_up, b_up, w_down, b_down, lg, lb):
    spare_base = _grouped_rows_bound(x1.shape[0])
    dispatch_tables, expert_tables, combine_tables = _routing_tables(counts[:, :, 0],
                                                                     spare_base)
    x_pad = _dispatch(dispatch_tables, x1, lslot, spare_base + 2 * LOCAL_ROWS)
    y_pad = _experts(layer, *expert_tables, x_pad, w_up, b_up, w_down, b_down)
    return _combine(combine_tables, x1, lslot, gates, lg, lb, y_pad)


def kernel(x, ab_w_in, a_norm_g, a_w_s, a_b_s, b_conv_w, ab_w_out, c_w_in, c_w_grp, c_scale,
           c_w_out, ln_mix_g, ln_mix_b, router_w, router_b, moe_w_up, moe_b_up, moe_w_down,
           moe_b_down, ln_ffn_g, ln_ffn_b):
    bsz, seq, d = x.shape
    assert d == D_MODEL and (bsz * seq) % MIX_TILE == 0 and seq % A_CHUNK == 0
    assert bsz == 1, "token tiles carry conv / pooling state across the sequence"
    h = x.reshape(bsz * seq, d)

    upper = jnp.triu(jnp.ones((ROUTE_TILE, ROUTE_TILE), BF16), k=1)
    lower = jnp.tril(jnp.ones((N_EXPERTS, N_EXPERTS), BF16), k=-1)
    causal = jnp.tril(jnp.ones((A_CHUNK, A_CHUNK), dtype=bool))
    b_up = moe_b_up.reshape(DEPTH, N_EXPERTS, 1, 2 * D_FF)
    b_down = moe_b_down.reshape(DEPTH, N_EXPERTS, 1, D_MODEL)

    for layer in range(DEPTH):
        i = layer // 2
        lg = ln_mix_g[layer].reshape(1, d)
        lb = ln_mix_b[layer].reshape(1, d)
        route_params = (_split_router_weight(router_w[layer]),
                        router_b[layer].reshape(N_EXPERTS, 1), upper, lower)
        if layer % 2 == 0:
            w_s = jnp.where(causal, a_w_s[i], 0.0).astype(BF16)
            wcat = w_s.reshape(A_HEADS // 2, 2, A_CHUNK, A_CHUNK).transpose(0, 2, 1, 3)
            wcat = wcat.reshape(A_HEADS // 2, A_CHUNK, 2 * A_CHUNK)
            bias = jnp.repeat(a_b_s[i].T, A_HEAD_DIM, axis=1)
            params = (ab_w_in[i].astype(BF16), a_norm_g[i].reshape(1, A_WIDTH), wcat, bias,
                      b_conv_w[i], ab_w_out[i].astype(BF16), lg, lb)
            mixed = _mixer_call(_even_mixer_kernel, "even_mixer",
                                pltpu.VMEM((MIX_TILE + CONV_HALO, B_WIDTH), F32),
                                h, params, route_params)
        else:
            params = (c_w_in[i].astype(BF16), c_w_grp[i].astype(BF16),
                      c_scale[i].reshape(1, d), c_w_out[i].astype(BF16), lg, lb)
            mixed = _mixer_call(_odd_mixer_kernel, "odd_mixer",
                                pltpu.VMEM((MIX_TILE + POOL_HALO, D_MODEL), F32),
                                h, params, route_params)
        h = _moe(layer, *mixed, moe_w_up, b_up, moe_w_down, b_down,
                 ln_ffn_g[layer].reshape(1, d), ln_ffn_b[layer].reshape(1, d))
    return h.reshape(bsz, seq, d)
```
